```python
import jax, jax.numpy as jnp
from jax import lax
import numpy as np

D_MODEL = 1024
BATCH = 16
SEQ = 2048
DEPTH = 2
DEC_BATCH = 32
DEC_SEQ = 1
PAST_LEN = 16384
PAGE_SIZE = 128

CHUNK = 128
A_GROUPS = 8
A_GROUP_DIM = 64
A_WIDTH = A_GROUPS * A_GROUP_DIM
HEAD_DIM = 64
HEADS_PER_PAIR = 4
DILATED_PAIRS = ((128, 1), (512, 4), (2048, 16))
N_PAIRS = len(DILATED_PAIRS)
B_HEADS = HEADS_PER_PAIR * N_PAIRS
B_WIDTH = B_HEADS * HEAD_DIM
B_OUT = HEADS_PER_PAIR * HEAD_DIM
QBLK = 128
D_FF = 2816
CONV_W = 3
ROPE_THETA = 10000.0
EPS = 1e-6
NEG_INF = -1e30
IN_COLS = 2 * A_WIDTH + 3 * B_WIDTH + 2 * D_MODEL

kernel_name = "hybrid_gmlp_dilated_attn_convffn_step"


def rmsnorm(x, g):
    xf = x.astype(jnp.float32)
    y = xf * lax.rsqrt(jnp.mean(xf * xf, axis=-1, keepdims=True) + EPS)
    return (y * g.astype(jnp.float32)).astype(x.dtype)


def apply_rope(x, pos):
    half = HEAD_DIM // 2
    inv_freq = ROPE_THETA ** (-jnp.arange(half, dtype=jnp.float32) / half)
    ang = pos.astype(jnp.float32)[:, None] * inv_freq[None, :]
    cos = jnp.cos(ang)[:, None, :]
    sin = jnp.sin(ang)[:, None, :]
    xf = x.astype(jnp.float32)
    x1, x2 = xf[..., :half], xf[..., half:]
    return jnp.concatenate([x1 * cos - x2 * sin, x2 * cos + x1 * sin], axis=-1).astype(x.dtype)


def masked_softmax_lse(scores, valid):
    s = jnp.where(valid, scores, NEG_INF)
    mx = jnp.max(s, axis=-1, keepdims=True)
    p = jnp.exp(s - mx)
    den = jnp.sum(p, axis=-1, keepdims=True)
    return p / den, (mx + jnp.log(den))[..., 0]


def chunk_spatial_gate(u, v, w_s, b_s):
    b, t, _ = v.shape
    tp = -(-t // CHUNK) * CHUNK
    nc = tp // CHUNK
    vp = jnp.pad(v, ((0, 0), (0, tp - t), (0, 0))).reshape(b, nc, CHUNK, A_GROUPS, A_GROUP_DIM)
    causal = jnp.tril(jnp.ones((CHUNK, CHUNK), dtype=bool))
    w = jnp.where(causal[None], w_s, 0)
    z = jnp.einsum('gts,bcsgd->bctgd', w, vp) + b_s.T[None, None, :, :, None]
    z = z.reshape(b, tp, A_WIDTH)[:, :t]
    return u * z


def dilated_attn_prompt(q, k, v, dil, taps):
    b, s, h, hd = q.shape
    m = s // dil
    mp = -(-m // QBLK) * QBLK
    nb = mp // QBLK

    def to_blocks(a):
        a = a.reshape(b, m, dil, h, hd).transpose(0, 2, 1, 3, 4)
        a = jnp.pad(a, ((0, 0), (0, 0), (0, mp - m), (0, 0), (0, 0)))
        return a.reshape(b, dil, nb, QBLK, h, hd)

    def with_prev(a):
        prev = jnp.pad(a, ((0, 0), (0, 0), (1, 0), (0, 0), (0, 0), (0, 0)))[:, :, :-1]
        return jnp.concatenate([prev, a], axis=3)

    qb = to_blocks(q)
    kk = with_prev(to_blocks(k))
    vv = with_prev(to_blocks(v))
    scores = jnp.einsum('brnqhd,brnkhd->brnhqk', qb, kk,
                        preferred_element_type=jnp.float32) * (HEAD_DIM ** -0.5)
    qi = jnp.arange(QBLK)[:, None]
    ki = jnp.arange(2 * QBLK)[None, :]
    dist = qi - ki + QBLK
    key_sub = jnp.arange(nb)[:, None, None] * QBLK + ki[None] - QBLK
    valid = (dist[None] >= 0) & (dist[None] <= taps) & (key_sub >= 0)
    attn, lse = masked_softmax_lse(scores, valid[None, None, :, None])
    out = jnp.einsum('brnhqk,brnkhd->brnqhd', attn, vv.astype(jnp.float32))
    out = out.reshape(b, dil, mp, h, hd)[:, :, :m].transpose(0, 2, 1, 3, 4).reshape(b, s, h, hd)
    lse = lse.transpose(0, 1, 2, 4, 3).reshape(b, dil, mp, h)[:, :, :m]
    lse = lse.transpose(0, 2, 1, 3).reshape(b, s, h)
    return out, lse


def dilated_attn_sample(q, kv_rows, kv_buf, dil, taps):
    b, t, h, hd = q.shape
    n_past = kv_buf.shape[1]
    all_kv = jnp.concatenate([kv_buf, kv_rows], axis=1)
    idx = n_past + jnp.arange(t)[:, None] - jnp.arange(taps + 1)[None, :] * dil
    valid = idx >= 0
    gathered = all_kv[:, jnp.maximum(idx, 0)]
    kg, vg = gathered[:, :, :, 0], gathered[:, :, :, 1]
    scores = jnp.einsum('bthd,btjhd->bthj', q, kg,
                        preferred_element_type=jnp.float32) * (HEAD_DIM ** -0.5)
    attn, lse = masked_softmax_lse(scores, valid[None, :, None, :])
    out = jnp.einsum('bthj,btjhd->bthd', attn, vg.astype(jnp.float32))
    return out, lse


def layer(x, pos, kv_bufs, conv_buf, g_mix, w_in, g_v, w_s, b_s, g_q, g_k,
          w_a_proj, w_b_proj, w_o, g_ffn, w_up, conv_w, conv_b, w_down):
    b, t, _ = x.shape
    xn = rmsnorm(x, g_mix)
    proj = xn @ w_in
    o1 = A_WIDTH
    o2 = o1 + A_WIDTH
    o3 = o2 + B_WIDTH
    o4 = o3 + B_WIDTH
    o5 = o4 + B_WIDTH
    o6 = o5 + D_MODEL
    u_a, v_a, q, k, v, gate_a, gate_b = jnp.split(proj, [o1, o2, o3, o4, o5, o6], axis=-1)

    u = jax.nn.gelu(u_a)
    va = rmsnorm(jax.nn.gelu(v_a), g_v)
    a_out = chunk_spatial_gate(u, va, w_s, b_s)

    q = apply_rope(rmsnorm(q.reshape(b, t, B_HEADS, HEAD_DIM), g_q), pos)
    k = apply_rope(rmsnorm(k.reshape(b, t, B_HEADS, HEAD_DIM), g_k), pos)
    v = v.reshape(b, t, B_HEADS, HEAD_DIM)
    outs, lses, new_kv = [], [], []
    for gi, (window, dil) in enumerate(DILATED_PAIRS):
        sl = slice(gi * HEADS_PER_PAIR, (gi + 1) * HEADS_PER_PAIR)
        qg, kg, vg = q[:, :, sl], k[:, :, sl], v[:, :, sl]
        taps = window // dil
        kv_rows = jnp.stack([kg, vg], axis=2)
        if kv_bufs is None:
            o, lse = dilated_attn_prompt(qg, kg, vg, dil, taps)
            new_kv.append(kv_rows[:, -min(window, t):])
        else:
            o, lse = dilated_attn_sample(qg, kv_rows, kv_bufs[gi], dil, taps)
            new_kv.append(kv_rows)
        outs.append(o)
        lses.append(lse)
    pair_w = jax.nn.softmax(jnp.stack(lses, axis=0), axis=0)
    b_out = sum(pair_w[gi][..., None] * outs[gi] for gi in range(N_PAIRS))
    b_out = b_out.astype(x.dtype).reshape(b, t, B_OUT)

    merged = jax.nn.sigmoid(gate_a) * (a_out @ w_a_proj) + jax.nn.sigmoid(gate_b) * (b_out @ w_b_proj)
    x = x + merged @ w_o

    up = rmsnorm(x, g_ffn) @ w_up
    if conv_buf is None:
        hist = jnp.zeros((b, CONV_W - 1, up.shape[-1]), dtype=up.dtype)
    else:
        hist = conv_buf.astype(up.dtype)
    padded = jnp.concatenate([hist, up], axis=1)
    c = conv_b + sum(conv_w[j] * padded[:, j:j + t] for j in range(CONV_W))
    new_conv = padded[:, -(CONV_W - 1):]
    c_gate, c_val = jnp.split(c, 2, axis=-1)
    x = x + (jax.nn.silu(c_gate) * c_val) @ w_down
    return x, new_kv, new_conv, va


def setup_inputs(seed: int = 0) -> dict:
    key = jax.random.key(seed)
    ks = jax.random.split(key, 24)

    def nrm(k, shape, scale):
        return jax.random.normal(k, shape, jnp.float32) * scale

    buf_len = [min(w, PAST_LEN) for (w, _) in DILATED_PAIRS]
    return {
        "x_prompt": nrm(ks[0], (BATCH, SEQ, D_MODEL), 1.0),
        "x_sample": nrm(ks[1], (DEC_BATCH, DEC_SEQ, D_MODEL), 1.0),
        "cache_kv_w128": nrm(ks[2], (DEPTH, DEC_BATCH, buf_len[0], 2, HEADS_PER_PAIR, HEAD_DIM), 1.0),
        "cache_kv_w512": nrm(ks[3], (DEPTH, DEC_BATCH, buf_len[1], 2, HEADS_PER_PAIR, HEAD_DIM), 1.0),
        "cache_kv_w2048": nrm(ks[4], (DEPTH, DEC_BATCH, buf_len[2], 2, HEADS_PER_PAIR, HEAD_DIM), 1.0),
        "state_conv": nrm(ks[5], (DEPTH, DEC_BATCH, CONV_W - 1, 2 * D_FF), 1.0),
        "g_mix": 1.0 + nrm(ks[6], (DEPTH, D_MODEL), 0.05),
        "w_in": nrm(ks[7], (DEPTH, D_MODEL, IN_COLS), D_MODEL ** -0.5),
        "g_v": 1.0 + nrm(ks[8], (DEPTH, A_WIDTH), 0.05),
        "w_s": nrm(ks[9], (DEPTH, A_GROUPS, CHUNK, CHUNK), CHUNK ** -0.5),
        "b_s": 1.0 + nrm(ks[10], (DEPTH, A_GROUPS, CHUNK), 0.1),
        "g_q": 1.0 + nrm(ks[11], (DEPTH, HEAD_DIM), 0.05),
        "g_k": 1.0 + nrm(ks[12], (DEPTH, HEAD_DIM), 0.05),
        "w_a_proj": nrm(ks[13], (DEPTH, A_WIDTH, D_MODEL), A_WIDTH ** -0.5),
        "w_b_proj": nrm(ks[14], (DEPTH, B_OUT, D_MODEL), B_OUT ** -0.5),
        "w_o": nrm(ks[15], (DEPTH, D_MODEL, D_MODEL), D_MODEL ** -0.5),
        "g_ffn": 1.0 + nrm(ks[16], (DEPTH, D_MODEL), 0.05),
        "w_up": nrm(ks[17], (DEPTH, D_MODEL, 2 * D_FF), D_MODEL ** -0.5),
        "conv_w": nrm(ks[18], (DEPTH, CONV_W, 2 * D_FF), CONV_W ** -0.5),
        "conv_b": nrm(ks[19], (DEPTH, 2 * D_FF), 0.02),
        "w_down": nrm(ks[20], (DEPTH, D_FF, D_MODEL), D_FF ** -0.5),
    }


def reference(x_prompt, x_sample, cache_kv_w128, cache_kv_w512, cache_kv_w2048, state_conv,
              g_mix, w_in, g_v, w_s, b_s, g_q, g_k, w_a_proj, w_b_proj, w_o,
              g_ffn, w_up, conv_w, conv_b, w_down):
    pos_prompt = jnp.arange(x_prompt.shape[1])
    pos_sample = PAST_LEN + jnp.arange(x_sample.shape[1])
    yp, ys = x_prompt, x_sample
    kvp = [[], [], []]
    kvs = [[], [], []]
    convp, convs, vrows = [], [], []
    for l in range(DEPTH):
        params = (g_mix[l], w_in[l], g_v[l], w_s[l], b_s[l], g_q[l], g_k[l],
                  w_a_proj[l], w_b_proj[l], w_o[l], g_ffn[l], w_up[l], conv_w[l], conv_b[l], w_down[l])
        yp, nkv_p, nconv_p, _ = layer(yp, pos_prompt, None, None, *params)
        ys, nkv_s, nconv_s, va_s = layer(
            ys, pos_sample, (cache_kv_w128[l], cache_kv_w512[l], cache_kv_w2048[l]),
            state_conv[l], *params)
        for gi in range(N_PAIRS):
            kvp[gi].append(nkv_p[gi])
            kvs[gi].append(nkv_s[gi])
        convp.append(nconv_p)
        convs.append(nconv_s)
        vrows.append(va_s)
    new_kv_w128_prompt = jnp.stack(kvp[0], axis=0)
    new_kv_w512_prompt = jnp.stack(kvp[1], axis=0)
    new_kv_w2048_prompt = jnp.stack(kvp[2], axis=0)
    new_conv_prompt = jnp.stack(convp, axis=0)
    new_kv_w128_sample = jnp.stack(kvs[0], axis=0)
    new_kv_w512_sample = jnp.stack(kvs[1], axis=0)
    new_kv_w2048_sample = jnp.stack(kvs[2], axis=0)
    new_conv_sample = jnp.stack(convs, axis=0)
    new_v_chunk_sample = jnp.stack(vrows, axis=0)
    return (yp, ys, new_kv_w128_prompt, new_kv_w512_prompt, new_kv_w2048_prompt, new_conv_prompt,
            new_kv_w128_sample, new_kv_w512_sample, new_kv_w2048_sample, new_conv_sample,
            new_v_chunk_sample)
```

```python
import functools

import jax
import jax.numpy as jnp
from jax import lax
from jax.experimental import pallas as pl
from jax.experimental.pallas import tpu as pltpu

D_MODEL = 1024
BATCH = 16
SEQ = 2048
DEPTH = 2
DEC_BATCH = 32
PAST_LEN = 16384
CHUNK = 128
A_GROUPS = 8
A_GROUP_DIM = 64
A_WIDTH = A_GROUPS * A_GROUP_DIM
HEAD_DIM = 64
HEADS_PER_PAIR = 4
DILATED_PAIRS = ((128, 1), (512, 4), (2048, 16))
N_PAIRS = len(DILATED_PAIRS)
PAIR_WIDTH = HEADS_PER_PAIR * HEAD_DIM
B_WIDTH = N_PAIRS * PAIR_WIDTH
QBLK = 128
D_FF = 2816
CONV_W = 3
ROPE_THETA = 10000.0
EPS = 1e-6
NEG_INF = -1e30
IN_COLS = 2 * A_WIDTH + 3 * B_WIDTH + 2 * D_MODEL
OFF_U, OFF_V = 0, A_WIDTH
OFF_Q = 2 * A_WIDTH
OFF_K = OFF_Q + B_WIDTH
OFF_VV = OFF_K + B_WIDTH
OFF_GA = OFF_VV + B_WIDTH
OFF_GB = OFF_GA + D_MODEL

V7X_LANES = 128
V7X_VMEM_BYTES = 64 * 1024 * 1024
VMEM_LIMIT = V7X_VMEM_BYTES - 8 * 1024 * 1024

TOK_TILE = 512
TILES_PER_SEQ = SEQ // TOK_TILE
HALO = 16
FF_CHUNK = 256
N_FF_CHUNKS = D_FF // FF_CHUNK
SAMPLE_GROUP = 8

F32 = jnp.float32
BF16 = jnp.bfloat16


def _rms(x, g):
    return x * lax.rsqrt(jnp.mean(x * x, axis=-1, keepdims=True) + EPS) * g


def _swap_halves(t):
    lane = lax.broadcasted_iota(jnp.int32, (1, V7X_LANES), 1)
    first_half = (lane % HEAD_DIM) < (HEAD_DIM // 2)
    slabs = []
    for j in range(t.shape[1] // V7X_LANES):
        s = t[:, j * V7X_LANES:(j + 1) * V7X_LANES]
        up = pltpu.roll(s, V7X_LANES - HEAD_DIM // 2, 1)
        down = pltpu.roll(s, HEAD_DIM // 2, 1)
        slabs.append(jnp.where(first_half, up, down))
    return jnp.concatenate(slabs, axis=1)


def _head_norm_rope(t, gain, mean_mat, cos, sin_signed):
    ms = jnp.dot((t * t).astype(BF16), mean_mat, preferred_element_type=F32)
    tn = t * lax.rsqrt(ms + EPS) * gain
    return tn * cos + _swap_halves(tn) * sin_signed


def _proj_kernel(*refs, prompt, rows):
    (x_ref, gmix_ref, w_ref, gv_ref, gq_ref, gk_ref, cos_ref, sin_ref, mean_ref) = refs[:9]
    outs = refs[9:]
    if prompt:
        (u_ref, va_ref, q0_ref, k0_ref, v0_ref, q1_ref, k1_ref, v1_ref, q2_ref, k2_ref, v2_ref,
         ga_ref, gb_ref, kv0_ref, kv1_ref, kv2_ref, scr_ref) = outs
        qkv_refs = ((q0_ref, k0_ref, v0_ref), (q1_ref, k1_ref, v1_ref), (q2_ref, k2_ref, v2_ref))
    else:
        (u_ref, va_ref, q0_ref, q1_ref, q2_ref, ga_ref, gb_ref, kv0_ref, kv1_ref, kv2_ref) = outs
        q_refs = (q0_ref, q1_ref, q2_ref)
    kv_refs = (kv0_ref, kv1_ref, kv2_ref)

    xn = _rms(x_ref[...], gmix_ref[...]).astype(BF16)

    def proj(c0, n):
        return jnp.dot(xn, w_ref[:, c0:c0 + n], preferred_element_type=F32)

    u_ref[...] = jax.nn.gelu(proj(OFF_U, A_WIDTH)).astype(u_ref.dtype)
    va_ref[...] = _rms(jax.nn.gelu(proj(OFF_V, A_WIDTH)), gv_ref[...]).astype(va_ref.dtype)
    ga_ref[...] = jax.nn.sigmoid(proj(OFF_GA, D_MODEL)).astype(ga_ref.dtype)
    gb_ref[...] = jax.nn.sigmoid(proj(OFF_GB, D_MODEL)).astype(gb_ref.dtype)

    cos = cos_ref[...]
    sin_signed = sin_ref[...]
    mean_mat = mean_ref[...]
    tile = pl.program_id(0) % TILES_PER_SEQ

    def store_residue_major(dst_ref, val, dil):
        n_slabs = PAIR_WIDTH // V7X_LANES
        for j in range(n_slabs):
            scr_ref[j] = val[:, j * V7X_LANES:(j + 1) * V7X_LANES]
        for r in range(dil):
            picked = [scr_ref[j, pl.ds(r, rows // dil, stride=dil), :] for j in range(n_slabs)]
            dst_ref[0, r] = jnp.concatenate(picked, axis=1).astype(dst_ref.dtype)

    for g, (window, dil) in enumerate(DILATED_PAIRS):
        c = g * PAIR_WIDTH
        q = _head_norm_rope(proj(OFF_Q + c, PAIR_WIDTH), gq_ref[...], mean_mat, cos, sin_signed)
        q = q * (HEAD_DIM ** -0.5)
        k = _head_norm_rope(proj(OFF_K + c, PAIR_WIDTH), gk_ref[...], mean_mat, cos, sin_signed)
        v = proj(OFF_VV + c, PAIR_WIDTH)
        kv_ref = kv_refs[g]
        if not prompt:
            q_refs[g][...] = q
            kv_ref[:, :PAIR_WIDTH] = k
            kv_ref[:, PAIR_WIDTH:] = v
            continue
        keep = min(window, SEQ)
        if keep >= rows:
            first_tile = (SEQ - keep) // rows

            @pl.when(tile >= first_tile)
            def _():
                kv_ref[0, 0, :, :PAIR_WIDTH] = k
                kv_ref[0, 0, :, PAIR_WIDTH:] = v
        else:
            @pl.when(tile == TILES_PER_SEQ - 1)
            def _():
                kv_ref[0, 0, :, :PAIR_WIDTH] = k[rows - keep:]
                kv_ref[0, 0, :, PAIR_WIDTH:] = v[rows - keep:]
        qr, kr, vr = qkv_refs[g]
        if dil == 1:
            qr[...] = q.astype(qr.dtype)
            kr[...] = k.astype(kr.dtype)
            vr[...] = v.astype(vr.dtype)
        else:
            store_residue_major(qr, q, dil)
            store_residue_major(kr, k, dil)
            store_residue_major(vr, v, dil)


def _full(shape):
    return pl.BlockSpec(shape, lambda i: (0,) * len(shape))


def _layer_block(shape, layer):
    return pl.BlockSpec((None,) + tuple(shape), lambda i: (layer,) + (0,) * len(shape))


def _proj_prompt(layer, x, g_mix, w_in, g_v, g_q, g_k, cos, sin_signed, mean_mat, kv_prev):
    rows = TOK_TILE
    n_tok = x.shape[0]
    grid = (n_tok // rows,)
    nt = TILES_PER_SEQ

    def tok(width):
        return pl.BlockSpec((rows, width), lambda i: (i, 0))

    in_specs = [
        tok(D_MODEL),
        _layer_block((1, D_MODEL), layer),
        _layer_block((D_MODEL, IN_COLS), layer),
        _layer_block((1, A_WIDTH), layer),
        _layer_block((1, PAIR_WIDTH), layer),
        _layer_block((1, PAIR_WIDTH), layer),
        pl.BlockSpec((rows, PAIR_WIDTH), lambda i: (i % nt, 0)),
        pl.BlockSpec((rows, PAIR_WIDTH), lambda i: (i % nt, 0)),
        _full((PAIR_WIDTH, PAIR_WIDTH)),
    ]
    out_shape = [jax.ShapeDtypeStruct((n_tok, A_WIDTH), BF16)] * 2
    out_specs = [tok(A_WIDTH)] * 2
    for (window, dil) in DILATED_PAIRS:
        if dil == 1:
            out_shape += [jax.ShapeDtypeStruct((n_tok, PAIR_WIDTH), BF16)] * 3
            out_specs += [tok(PAIR_WIDTH)] * 3
        else:
            out_shape += [jax.ShapeDtypeStruct((BATCH, dil, SEQ // dil, PAIR_WIDTH), BF16)] * 3
            out_specs += [pl.BlockSpec((1, dil, rows // dil, PAIR_WIDTH),
                                       lambda i: (i // nt, 0, i % nt, 0))] * 3
    out_shape += [jax.ShapeDtypeStruct((n_tok, D_MODEL), BF16)] * 2
    out_specs += [tok(D_MODEL)] * 2
    n_plain_out = len(out_shape)
    for (window, dil) in DILATED_PAIRS:
        keep = min(window, SEQ)
        out_shape.append(jax.ShapeDtypeStruct((DEPTH, BATCH, keep, 2 * PAIR_WIDTH), F32))
        blk = min(keep, rows)
        first_tile = (SEQ - max(keep, rows)) // rows
        out_specs.append(pl.BlockSpec(
            (1, 1, blk, 2 * PAIR_WIDTH),
            functools.partial(lambda i, ft: (layer, i // nt, jnp.maximum(i % nt - ft, 0), 0),
                              ft=first_tile)))
    args = [x, g_mix, w_in, g_v, g_q, g_k, cos, sin_signed, mean_mat]
    aliases = {}
    if kv_prev is not None:
        for j, buf in enumerate(kv_prev):
            in_specs.append(pl.BlockSpec(memory_space=pl.ANY))
            aliases[len(args)] = n_plain_out + j
            args.append(buf)

    def body(*refs):
        n_in = 9
        ins = refs[:n_in]
        rest = refs[n_in + len(aliases):]
        _proj_kernel(*ins, *rest, prompt=True, rows=rows)

    return pl.pallas_call(
        body,
        grid=grid,
        in_specs=in_specs,
        out_specs=out_specs,
        out_shape=out_shape,
        scratch_shapes=[pltpu.VMEM((PAIR_WIDTH // V7X_LANES, rows, V7X_LANES), F32)],
        input_output_aliases=aliases,
        compiler_params=pltpu.CompilerParams(
            dimension_semantics=("arbitrary",), vmem_limit_bytes=VMEM_LIMIT),
        name=f"proj_prompt_l{layer}",
    )(*args)


def _proj_sample(layer, x, g_mix, w_in, g_v, g_q, g_k, cos, sin_signed, mean_mat):
    rows = x.shape[0]
    in_specs = [
        _full((rows, D_MODEL)),
        _layer_block((1, D_MODEL), layer),
        _layer_block((D_MODEL, IN_COLS), layer),
        _layer_block((1, A_WIDTH), layer),
        _layer_block((1, PAIR_WIDTH), layer),
        _layer_block((1, PAIR_WIDTH), layer),
        _full((rows, PAIR_WIDTH)),
        _full((rows, PAIR_WIDTH)),
        _full((PAIR_WIDTH, PAIR_WIDTH)),
    ]
    widths = [A_WIDTH, A_WIDTH] + [PAIR_WIDTH] * 3 + [D_MODEL] * 2 + [2 * PAIR_WIDTH] * 3
    out_shape = [jax.ShapeDtypeStruct((rows, w), F32) for w in widths]
    out_specs = [_full((rows, w)) for w in widths]
    return pl.pallas_call(
        functools.partial(_proj_kernel, prompt=False, rows=rows),
        grid=(1,),
        in_specs=in_specs,
        out_specs=out_specs,
        out_shape=out_shape,
        compiler_params=pltpu.CompilerParams(
            dimension_semantics=("arbitrary",), vmem_limit_bytes=VMEM_LIMIT),
        name=f"proj_sample_l{layer}",
    )(x, g_mix, w_in, g_v, g_q, g_k, cos, sin_signed, mean_mat)


def _attn_block(q, kk, vv, valid):
    lane_head = lax.broadcasted_iota(jnp.int32, (1, PAIR_WIDTH), 1) // HEAD_DIM
    o = jnp.zeros((QBLK, PAIR_WIDTH), F32)
    lse = jnp.zeros((QBLK, PAIR_WIDTH), F32)
    for h in range(HEADS_PER_PAIR):
        hm = lane_head == h
        qh = jnp.where(hm, q, jnp.zeros_like(q))
        s = lax.dot_general(qh, kk, (((1,), (1,)), ((), ())), preferred_element_type=F32)
        s = jnp.where(valid, s, NEG_INF)
        mx = jnp.max(s, axis=-1, keepdims=True)
        p = jnp.exp(s - mx)
        den = jnp.sum(p, axis=-1, keepdims=True)
        pv = jnp.dot(p.astype(BF16), vv, preferred_element_type=F32)
        o = jnp.where(hm, pv / den, o)
        lse = jnp.where(hm, mx + jnp.log(den), lse)
    return o, lse


def _attn_kernel(q0_ref, k0_ref, v0_ref, q1_ref, k1_ref, v1_ref, q2_ref, k2_ref, v2_ref,
                 out_ref, o_scr, l_scr):
    qi = lax.broadcasted_iota(jnp.int32, (QBLK, QBLK), 0)
    ki = lax.broadcasted_iota(jnp.int32, (QBLK, QBLK), 1)
    causal = ki <= qi
    qi2 = lax.broadcasted_iota(jnp.int32, (QBLK, 2 * QBLK), 0)
    ki2 = lax.broadcasted_iota(jnp.int32, (QBLK, 2 * QBLK), 1)
    dist = qi2 - ki2 + QBLK
    taps = DILATED_PAIRS[0][0] // DILATED_PAIRS[0][1]
    assert all(w // d == taps for (w, d) in DILATED_PAIRS) and taps >= QBLK - 1
    band = (dist >= 0) & (dist <= taps)

    def run_pair(g, dil, load, store):
        n_blocks = (SEQ // dil) // QBLK

        def residue(r):
            o, lse = _attn_block(load(0, r, 0, QBLK), load(1, r, 0, QBLK), load(2, r, 0, QBLK),
                                 causal)
            store(r, 0, o, lse)

            def later(i, carry):
                prev = pl.multiple_of((i - 1) * QBLK, QBLK)
                cur = pl.multiple_of(i * QBLK, QBLK)
                o, lse = _attn_block(load(0, r, cur, QBLK), load(1, r, prev, 2 * QBLK),
                                     load(2, r, prev, 2 * QBLK), band)
                store(r, i, o, lse)
                return carry

            if n_blocks > 1:
                lax.fori_loop(1, n_blocks, later, 0)

        if dil == 1:
            residue(0)
        else:
            def body(r, carry):
                residue(r)
                return carry
            lax.fori_loop(0, dil, body, 0)

    refs0 = (q0_ref, k0_ref, v0_ref)

    def load0(which, r, start, n):
        return refs0[which][pl.ds(start, n), :]

    n_slabs = PAIR_WIDTH // V7X_LANES

    def put(scr, g, rows, val):
        for j in range(n_slabs):
            scr[g, j, rows, :] = val[:, j * V7X_LANES:(j + 1) * V7X_LANES]

    def get(scr, g, rows):
        return jnp.concatenate([scr[g, j, rows, :] for j in range(n_slabs)], axis=1)

    def store0(r, i, o, lse):
        rows = pl.ds(pl.multiple_of(i * QBLK, QBLK), QBLK)
        put(o_scr, 0, rows, o)
        put(l_scr, 0, rows, lse)

    run_pair(0, 1, load0, store0)

    for g, refs in ((1, (q1_ref, k1_ref, v1_ref)), (2, (q2_ref, k2_ref, v2_ref))):
        dil = DILATED_PAIRS[g][1]

        def load(which, r, start, n, refs=refs):
            return refs[which][0, r, pl.ds(start, n), :]

        def store(r, i, o, lse, g=g, dil=dil):
            rows = pl.ds(r + i * (QBLK * dil), QBLK, stride=dil)
            put(o_scr, g, rows, o)
            put(l_scr, g, rows, lse)

        run_pair(g, dil, load, store)

    step = 256
    for t0 in range(0, SEQ, step):
        rows = slice(t0, t0 + step)
        l0, l1, l2 = get(l_scr, 0, rows), get(l_scr, 1, rows), get(l_scr, 2, rows)
        m = jnp.maximum(jnp.maximum(l0, l1), l2)
        e0, e1, e2 = jnp.exp(l0 - m), jnp.exp(l1 - m), jnp.exp(l2 - m)
        den = e0 + e1 + e2
        merged = ((e0 / den) * get(o_scr, 0, rows) + (e1 / den) * get(o_scr, 1, rows)
                  + (e2 / den) * get(o_scr, 2, rows))
        out_ref[rows, :] = merged.astype(out_ref.dtype)


def _attn_prompt(layer, qkv):
    n_tok = qkv[0].shape[0]
    in_specs = []
    for (window, dil) in DILATED_PAIRS:
        if dil == 1:
            in_specs += [pl.BlockSpec((SEQ, PAIR_WIDTH), lambda b: (b, 0))] * 3
        else:
            in_specs += [pl.BlockSpec((1, dil, SEQ // dil, PAIR_WIDTH),
                                      lambda b: (b, 0, 0, 0))] * 3
    return pl.pallas_call(
        _attn_kernel,
        grid=(BATCH,),
        in_specs=in_specs,
        out_specs=pl.BlockSpec((SEQ, PAIR_WIDTH), lambda b: (b, 0)),
        out_shape=jax.ShapeDtypeStruct((n_tok, PAIR_WIDTH), BF16),
        scratch_shapes=[pltpu.VMEM((N_PAIRS, PAIR_WIDTH // V7X_LANES, SEQ, V7X_LANES), F32),
                        pltpu.VMEM((N_PAIRS, PAIR_WIDTH // V7X_LANES, SEQ, V7X_LANES), F32)],
        compiler_params=pltpu.CompilerParams(
            dimension_semantics=("arbitrary",), vmem_limit_bytes=VMEM_LIMIT),
        name=f"attn_prompt_l{layer}",
    )(*qkv)


def _gated_merge(x, a_out, b_out, ga, gb, wa_ref, wb_ref, wo_ref):
    ap = jnp.dot(a_out.astype(BF16), wa_ref[...], preferred_element_type=F32)
    bp = jnp.dot(b_out.astype(BF16), wb_ref[...], preferred_element_type=F32)
    merged = ga * ap + gb * bp
    return x + jnp.dot(merged.astype(BF16), wo_ref[...], preferred_element_type=F32)


def _mix_kernel(x_ref, u_ref, va_ref, bo_ref, ga_ref, gb_ref, ws_ref, bias_ref,
                wa_ref, wb_ref, wo_ref, out_ref, a_scr):
    ti = lax.broadcasted_iota(jnp.int32, (CHUNK, CHUNK), 0)
    si = lax.broadcasted_iota(jnp.int32, (CHUNK, CHUNK), 1)
    tril = si <= ti
    w_tril = [jnp.where(tril, ws_ref[g], 0.0).astype(BF16) for g in range(A_GROUPS)]
    low_group = lax.broadcasted_iota(jnp.int32, (1, V7X_LANES), 1) < A_GROUP_DIM
    bias = bias_ref[...]
    groups_per_slab = V7X_LANES // A_GROUP_DIM
    for c in range(TOK_TILE // CHUNK):
        rows = slice(c * CHUNK, (c + 1) * CHUNK)
        zs = []
        for j in range(A_WIDTH // V7X_LANES):
            slab = va_ref[rows, j * V7X_LANES:(j + 1) * V7X_LANES]
            z_lo = jnp.dot(w_tril[groups_per_slab * j], slab, preferred_element_type=F32)
            z_hi = jnp.dot(w_tril[groups_per_slab * j + 1], slab, preferred_element_type=F32)
            zs.append(jnp.where(low_group, z_lo, z_hi))
        z = jnp.concatenate(zs, axis=1) + bias
        a_scr[rows, :] = (u_ref[rows, :].astype(F32) * z).astype(BF16)
    out_ref[...] = _gated_merge(x_ref[...], a_scr[...], bo_ref[...],
                                ga_ref[...].astype(F32), gb_ref[...].astype(F32),
                                wa_ref, wb_ref, wo_ref)


def _mix_prompt(layer, x, u, va, b_out, ga, gb, w_s, bias_full, w_a, w_b, w_o):
    rows = TOK_TILE
    n_tok = x.shape[0]

    def tok(width):
        return pl.BlockSpec((rows, width), lambda i: (i, 0))

    in_specs = [
        tok(D_MODEL), tok(A_WIDTH), tok(A_WIDTH), tok(PAIR_WIDTH), tok(D_MODEL), tok(D_MODEL),
        _layer_block((A_GROUPS, CHUNK, CHUNK), layer),
        _layer_block((CHUNK, A_WIDTH), layer),
        _layer_block((A_WIDTH, D_MODEL), layer),
        _layer_block((PAIR_WIDTH, D_MODEL), layer),
        _layer_block((D_MODEL, D_MODEL), layer),
    ]
    return pl.pallas_call(
        _mix_kernel,
        grid=(n_tok // rows,),
        in_specs=in_specs,
        out_specs=tok(D_MODEL),
        out_shape=jax.ShapeDtypeStruct((n_tok, D_MODEL), F32),
        scratch_shapes=[pltpu.VMEM((rows, A_WIDTH), BF16)],
        compiler_params=pltpu.CompilerParams(
            dimension_semantics=("arbitrary",), vmem_limit_bytes=VMEM_LIMIT),
        name=f"mix_prompt_l{layer}",
    )(x, u, va, b_out, ga, gb, w_s, bias_full, w_a, w_b, w_o)


def _ffn_kernel(x_ref, halo_ref, g_ref, wup_ref, cw_ref, cb_ref, wdown_ref,
                out_ref, conv_ref, xn_scr, h_scr):
    rows = TOK_TILE
    tile = pl.program_id(0) % TILES_PER_SEQ
    x = x_ref[...]
    g = g_ref[...]
    halo_on = jnp.where(tile == 0, 0.0, 1.0)
    xn_scr[:HALO, :] = (_rms(halo_ref[...], g) * halo_on).astype(BF16)
    xn_scr[HALO:, :] = _rms(x, g).astype(BF16)
    xn = xn_scr[...]

    def up_and_conv(c0):
        cols = slice(c0, c0 + FF_CHUNK)
        up = jnp.dot(xn, wup_ref[:, cols], preferred_element_type=F32)
        conv = cb_ref[:, cols]
        for j in range(CONV_W):
            lo = HALO - (CONV_W - 1) + j
            conv = conv + cw_ref[j:j + 1, cols] * up[lo:lo + rows]

        @pl.when(tile == TILES_PER_SEQ - 1)
        def _():
            conv_ref[0, :, cols] = up[HALO + rows - (CONV_W - 1):]
        return conv

    for j in range(N_FF_CHUNKS):
        c_gate = up_and_conv(j * FF_CHUNK)
        c_val = up_and_conv(D_FF + j * FF_CHUNK)
        h_scr[:, j * FF_CHUNK:(j + 1) * FF_CHUNK] = (jax.nn.silu(c_gate) * c_val).astype(BF16)
    out_ref[...] = x + jnp.dot(h_scr[...], wdown_ref[...], preferred_element_type=F32)


def _ffn_prompt(layer, x, g_ffn, w_up, conv_w, conv_b, w_down):
    rows = TOK_TILE
    n_tok = x.shape[0]
    nt = TILES_PER_SEQ
    halo_per_tile = rows // HALO
    in_specs = [
        pl.BlockSpec((rows, D_MODEL), lambda i: (i, 0)),
        pl.BlockSpec((HALO, D_MODEL), lambda i: (jnp.maximum(i * halo_per_tile - 1, 0), 0)),
        _layer_block((1, D_MODEL), layer),
        _layer_block((D_MODEL, 2 * D_FF), layer),
        _layer_block((CONV_W, 2 * D_FF), layer),
        _layer_block((1, 2 * D_FF), layer),
        _layer_block((D_FF, D_MODEL), layer),
    ]
    return pl.pallas_call(
        _ffn_kernel,
        grid=(n_tok // rows,),
        in_specs=in_specs,
        out_specs=[pl.BlockSpec((rows, D_MODEL), lambda i: (i, 0)),
                   pl.BlockSpec((1, CONV_W - 1, 2 * D_FF), lambda i: (i // nt, 0, 0))],
        out_shape=[jax.ShapeDtypeStruct((n_tok, D_MODEL), F32),
                   jax.ShapeDtypeStruct((BATCH, CONV_W - 1, 2 * D_FF), F32)],
        scratch_shapes=[pltpu.VMEM((HALO + rows, D_MODEL), BF16),
                        pltpu.VMEM((rows, D_FF), BF16)],
        compiler_params=pltpu.CompilerParams(
            dimension_semantics=("arbitrary",), vmem_limit_bytes=VMEM_LIMIT),
        name=f"ffn_prompt_l{layer}",
    )(x, x, g_ffn, w_up, conv_w, conv_b, w_down)


def _head_sum(a, ones_mat):
    hi = a.astype(BF16)
    lo = (a - hi.astype(F32)).astype(BF16)
    return (jnp.dot(hi, ones_mat, preferred_element_type=F32)
            + jnp.dot(lo, ones_mat, preferred_element_type=F32))


def _sample_mix_kernel(q0_ref, q1_ref, q2_ref, kvn0_ref, kvn1_ref, kvn2_ref,
                       c0_ref, c1_ref, c2_ref, x_ref, u_ref, va_ref, ga_ref, gb_ref,
                       wdiag_ref, bias0_ref, ones_ref, wa_ref, wb_ref, wo_ref, out_ref,
                       o_scr, l_scr):
    ones_mat = ones_ref[...]
    outs, lses = [], []
    for g, (q_ref, kvn_ref, c_ref) in enumerate(((q0_ref, kvn0_ref, c0_ref),
                                                 (q1_ref, kvn1_ref, c1_ref),
                                                 (q2_ref, kvn2_ref, c2_ref))):
        q_all = q_ref[...]
        k_new = kvn_ref[:, :PAIR_WIDTH]
        v_new = kvn_ref[:, PAIR_WIDTH:]
        s_new_all = _head_sum(q_all * k_new, ones_mat)
        for b in range(SAMPLE_GROUP):
            row = slice(b, b + 1)
            k_cache = c_ref[0, b, :, :PAIR_WIDTH]
            v_cache = c_ref[0, b, :, PAIR_WIDTH:]
            s = _head_sum(k_cache * q_all[row], ones_mat)
            s_new = s_new_all[row]
            mx = jnp.maximum(jnp.max(s, axis=0, keepdims=True), s_new)
            p = jnp.exp(s - mx)
            p_new = jnp.exp(s_new - mx)
            den = jnp.sum(p, axis=0, keepdims=True) + p_new
            o_scr[g, row, :] = (jnp.sum(p * v_cache, axis=0, keepdims=True)
                                + p_new * v_new[row]) / den
            l_scr[g, row, :] = mx + jnp.log(den)
        outs.append(o_scr[g])
        lses.append(l_scr[g])
    m = jnp.maximum(jnp.maximum(lses[0], lses[1]), lses[2])
    es = [jnp.exp(l - m) for l in lses]
    den = es[0] + es[1] + es[2]
    b_out = sum((e / den) * o for e, o in zip(es, outs))
    a_out = u_ref[...] * (wdiag_ref[...] * va_ref[...] + bias0_ref[...])
    out_ref[...] = _gated_merge(x_ref[...], a_out, b_out, ga_ref[...], gb_ref[...],
                                wa_ref, wb_ref, wo_ref)


def _mix_sample(layer, x, u, va, qs, kvns, caches, ga, gb, wdiag, bias0, ones_mat, w_a, w_b, w_o):
    n = SAMPLE_GROUP
    rows = x.shape[0]

    def grp(width):
        return pl.BlockSpec((n, width), lambda i: (i, 0))

    in_specs = [grp(PAIR_WIDTH)] * 3 + [grp(2 * PAIR_WIDTH)] * 3
    for (window, dil) in DILATED_PAIRS:
        in_specs.append(pl.BlockSpec((1, n, window // dil, 2 * PAIR_WIDTH),
                                     lambda i: (layer, i, 0, 0)))
    in_specs += [grp(D_MODEL), grp(A_WIDTH), grp(A_WIDTH), grp(D_MODEL), grp(D_MODEL),
                 _layer_block((1, A_WIDTH), layer), _layer_block((1, A_WIDTH), layer),
                 _full((PAIR_WIDTH, PAIR_WIDTH)),
                 _layer_block((A_WIDTH, D_MODEL), layer),
                 _layer_block((PAIR_WIDTH, D_MODEL), layer),
                 _layer_block((D_MODEL, D_MODEL), layer)]
    return pl.pallas_call(
        _sample_mix_kernel,
        grid=(rows // n,),
        in_specs=in_specs,
        out_specs=grp(D_MODEL),
        out_shape=jax.ShapeDtypeStruct((rows, D_MODEL), F32),
        scratch_shapes=[pltpu.VMEM((N_PAIRS, n, PAIR_WIDTH), F32),
                        pltpu.VMEM((N_PAIRS, n, PAIR_WIDTH), F32)],
        compiler_params=pltpu.CompilerParams(
            dimension_semantics=("arbitrary",), vmem_limit_bytes=VMEM_LIMIT),
        name=f"mix_sample_l{layer}",
    )(*qs, *kvns, *caches, x, u, va, ga, gb, wdiag, bias0, ones_mat, w_a, w_b, w_o)


def _sample_ffn_kernel(x_ref, g_ref, wg_ref, wv_ref, cwg_ref, cwv_ref, cbg_ref, cbv_ref,
                       h0g_ref, h0v_ref, h1g_ref, h1v_ref, wdown_ref,
                       out_ref, upg_ref, upv_ref):
    j = pl.program_id(0)
    x = x_ref[...]
    xn = _rms(x, g_ref[...]).astype(BF16)

    def up_and_conv(w_ref, cw_ref, cb_ref, h0_ref, h1_ref, up_ref):
        up = jnp.dot(xn, w_ref[...], preferred_element_type=F32)
        up_ref[...] = up
        return (cb_ref[...] + cw_ref[0:1, :] * h0_ref[...] + cw_ref[1:2, :] * h1_ref[...]
                + cw_ref[2:3, :] * up)

    c_gate = up_and_conv(wg_ref, cwg_ref, cbg_ref, h0g_ref, h1g_ref, upg_ref)
    c_val = up_and_conv(wv_ref, cwv_ref, cbv_ref, h0v_ref, h1v_ref, upv_ref)
    h = (jax.nn.silu(c_gate) * c_val).astype(BF16)
    part = jnp.dot(h, wdown_ref[...], preferred_element_type=F32)

    @pl.when(j == 0)
    def _():
        out_ref[...] = x + part

    @pl.when(j > 0)
    def _():
        out_ref[...] += part


def _ffn_sample(layer, x, g_ffn, w_up, conv_w, conv_b, state_conv, w_down):
    rows = x.shape[0]
    nch = N_FF_CHUNKS
    fc = FF_CHUNK

    def cols(nrows, half):
        return pl.BlockSpec((None, nrows, fc), lambda j: (layer, 0, half * nch + j))

    state_flat = state_conv.reshape(DEPTH, rows, (CONV_W - 1) * 2 * D_FF)
    hist = [cols(rows, 2 * row + half) for row in range(CONV_W - 1) for half in range(2)]
    in_specs = [
        _full((rows, D_MODEL)),
        _layer_block((1, D_MODEL), layer),
        cols(D_MODEL, 0), cols(D_MODEL, 1),
        cols(CONV_W, 0), cols(CONV_W, 1),
        cols(1, 0), cols(1, 1),
        *hist,
        pl.BlockSpec((None, fc, D_MODEL), lambda j: (layer, j, 0)),
    ]
    up_spec = pl.BlockSpec((rows, fc), lambda j: (0, j))
    up_shape = jax.ShapeDtypeStruct((rows, D_FF), F32)
    return pl.pallas_call(
        _sample_ffn_kernel,
        grid=(nch,),
        in_specs=in_specs,
        out_specs=[_full((rows, D_MODEL)), up_spec, up_spec],
        out_shape=[jax.ShapeDtypeStruct((rows, D_MODEL), F32), up_shape, up_shape],
        compiler_params=pltpu.CompilerParams(
            dimension_semantics=("arbitrary",), vmem_limit_bytes=VMEM_LIMIT),
        name=f"ffn_sample_l{layer}",
    )(x, g_ffn, w_up, w_up, conv_w, conv_w, conv_b, conv_b,
      state_flat, state_flat, state_flat, state_flat, w_down)


def _rope_tables(pos):
    half = HEAD_DIM // 2
    inv_freq = ROPE_THETA ** (-jnp.arange(half, dtype=F32) / half)
    ang = pos.astype(F32)[:, None] * inv_freq[None, :]
    cos, sin = jnp.cos(ang), jnp.sin(ang)
    cos_full = jnp.tile(jnp.concatenate([cos, cos], axis=1), (1, HEADS_PER_PAIR))
    sin_signed = jnp.tile(jnp.concatenate([-sin, sin], axis=1), (1, HEADS_PER_PAIR))
    return cos_full, sin_signed


def kernel(x_prompt, x_sample, cache_kv_w128, cache_kv_w512, cache_kv_w2048, state_conv,
           g_mix, w_in, g_v, w_s, b_s, g_q, g_k, w_a_proj, w_b_proj, w_o,
           g_ffn, w_up, conv_w, conv_b, w_down):
    assert x_prompt.shape == (BATCH, SEQ, D_MODEL) and x_sample.shape == (DEC_BATCH, 1, D_MODEL)
    n_dec = DEC_BATCH

    w_in_b, w_a_b, w_b_b, w_o_b = (w.astype(BF16) for w in (w_in, w_a_proj, w_b_proj, w_o))
    w_up_b, w_down_b = w_up.astype(BF16), w_down.astype(BF16)
    g_mix3 = g_mix[:, None, :]
    g_v3 = g_v[:, None, :]
    g_ffn3 = g_ffn[:, None, :]
    g_q3 = jnp.tile(g_q, (1, HEADS_PER_PAIR))[:, None, :]
    g_k3 = jnp.tile(g_k, (1, HEADS_PER_PAIR))[:, None, :]
    conv_b3 = conv_b[:, None, :]
    bias_full = jnp.repeat(jnp.swapaxes(b_s, 1, 2), A_GROUP_DIM, axis=2)
    bias0 = bias_full[:, 0:1, :]
    wdiag = jnp.repeat(w_s[:, :, 0, 0], A_GROUP_DIM, axis=1)[:, None, :]
    head_of = jnp.arange(PAIR_WIDTH) // HEAD_DIM
    same_head = head_of[:, None] == head_of[None, :]
    mean_mat = jnp.where(same_head, 1.0 / HEAD_DIM, 0.0).astype(BF16)
    ones_mat = jnp.where(same_head, 1.0, 0.0).astype(BF16)
    cos_p, sin_p = _rope_tables(jnp.arange(SEQ))
    cos_s, sin_s = _rope_tables(jnp.full((n_dec,), PAST_LEN))

    caches = []
    for cache, (window, dil) in zip((cache_kv_w128, cache_kv_w512, cache_kv_w2048), DILATED_PAIRS):
        assert cache.shape[2] == window
        caches.append(cache.reshape(DEPTH, n_dec, window // dil, dil * 2 * PAIR_WIDTH))

    xp = x_prompt.reshape(BATCH * SEQ, D_MODEL)
    xs = x_sample.reshape(n_dec, D_MODEL)
    kv_prompt = None
    conv_p, conv_s, kv_s, va_s = [], [], [[], [], []], []
    for l in range(DEPTH):
        outs = _proj_prompt(l, xp, g_mix3, w_in_b, g_v3, g_q3, g_k3, cos_p, sin_p, mean_mat,
                            kv_prompt)
        u, va = outs[0], outs[1]
        qkv = outs[2:11]
        ga, gb = outs[11], outs[12]
        kv_prompt = outs[13:16]
        b_out = _attn_prompt(l, qkv)
        xp = _mix_prompt(l, xp, u, va, b_out, ga, gb, w_s, bias_full, w_a_b, w_b_b, w_o_b)
        xp, nconv = _ffn_prompt(l, xp, g_ffn3, w_up_b, conv_w, conv_b3, w_down_b)
        conv_p.append(nconv)

        (su, sva, sq0, sq1, sq2, sga, sgb, skv0, skv1, skv2) = _proj_sample(
            l, xs, g_mix3, w_in_b, g_v3, g_q3, g_k3, cos_s, sin_s, mean_mat)
        xs = _mix_sample(l, xs, su, sva, (sq0, sq1, sq2), (skv0, skv1, skv2), caches, sga, sgb,
                         wdiag, bias0, ones_mat, w_a_b, w_b_b, w_o_b)
        xs, upg, upv = _ffn_sample(l, xs, g_ffn3, w_up_b, conv_w, conv_b3, state_conv, w_down_b)
        up_row = jnp.concatenate([upg, upv], axis=-1)[:, None, :]
        conv_s.append(jnp.concatenate([state_conv[l][:, 1:], up_row], axis=1))
        for g, skv in enumerate((skv0, skv1, skv2)):
            kv_s[g].append(skv)
        va_s.append(sva)

    def kv_shape(a, rows):
        return a.reshape(DEPTH, -1, rows, 2, HEADS_PER_PAIR, HEAD_DIM)

    return (
        xp.reshape(BATCH, SEQ, D_MODEL),
        xs.reshape(n_dec, 1, D_MODEL),
        kv_shape(kv_prompt[0], min(DILATED_PAIRS[0][0], SEQ)),
        kv_shape(kv_prompt[1], min(DILATED_PAIRS[1][0], SEQ)),
        kv_shape(kv_prompt[2], min(DILATED_PAIRS[2][0], SEQ)),
        jnp.stack(conv_p, axis=0),
        kv_shape(jnp.stack(kv_s[0], axis=0), 1),
        kv_shape(jnp.stack(kv_s[1], axis=0), 1),
        kv_shape(jnp.stack(kv_s[2], axis=0), 1),
        jnp.stack(conv_s, axis=0),
        jnp.stack(va_s, axis=0).reshape(DEPTH, n_dec, 1, A_WIDTH),
    )
```

```python
import functools

import jax
import jax.numpy as jnp
from jax import lax
from jax.experimental import pallas as pl
from jax.experimental.pallas import tpu as pltpu

D_MODEL = 1024
BATCH = 16
SEQ = 2048
DEPTH = 2
DEC_BATCH = 32
PAST_LEN = 16384
CHUNK = 128
A_GROUPS = 8
A_GROUP_DIM = 64
A_WIDTH = A_GROUPS * A_GROUP_DIM
HEAD_DIM = 64
HEADS_PER_PAIR = 4
DILATED_PAIRS = ((128, 1), (512, 4), (2048, 16))
N_PAIRS = len(DILATED_PAIRS)
TAPS = DILATED_PAIRS[0][0] // DILATED_PAIRS[0][1]
PAIR_WIDTH = HEADS_PER_PAIR * HEAD_DIM
B_WIDTH = N_PAIRS * PAIR_WIDTH
QBLK = 128
D_FF = 2816
CONV_W = 3
ROPE_THETA = 10000.0
EPS = 1e-6
NEG_INF = -1e30
OFF_U, OFF_V = 0, A_WIDTH
OFF_Q = 2 * A_WIDTH
OFF_K = OFF_Q + B_WIDTH
OFF_VV = OFF_K + B_WIDTH
OFF_GATES = OFF_VV + B_WIDTH
MAIN_COLS = OFF_GATES

V7X_LANES = 128
V7X_VMEM_BYTES = 64 * 1024 * 1024
VMEM_LIMIT = V7X_VMEM_BYTES - 8 * 1024 * 1024

TOK_TILE = 512
TILES_PER_SEQ = SEQ // TOK_TILE
HALO = 16
FF_CHUNK = 256
N_FF_CHUNKS = D_FF // FF_CHUNK
ATTN_UNROLL = 4
SAMPLE_GROUP = 2

F32 = jnp.float32
BF16 = jnp.bfloat16

assert all(w // d == TAPS for (w, d) in DILATED_PAIRS) and TAPS >= QBLK - 1


def _params(**kw):
    return pltpu.CompilerParams(dimension_semantics=("arbitrary",),
                                vmem_limit_bytes=VMEM_LIMIT, **kw)


def _rms(x, g):
    return x * lax.rsqrt(jnp.mean(x * x, axis=-1, keepdims=True) + EPS) * g


def _swap_halves(t):
    lane = lax.broadcasted_iota(jnp.int32, (1, V7X_LANES), 1)
    first_half = (lane % HEAD_DIM) < (HEAD_DIM // 2)
    slabs = []
    for j in range(t.shape[1] // V7X_LANES):
        s = t[:, j * V7X_LANES:(j + 1) * V7X_LANES]
        up = pltpu.roll(s, V7X_LANES - HEAD_DIM // 2, 1)
        down = pltpu.roll(s, HEAD_DIM // 2, 1)
        slabs.append(jnp.where(first_half, up, down))
    return jnp.concatenate(slabs, axis=1)


def _head_norm_rope(t, gain, mean_mat, cos, sin_signed):
    ms = jnp.dot((t * t).astype(BF16), mean_mat, preferred_element_type=F32)
    tn = t * lax.rsqrt(ms + EPS) * gain
    return tn * cos + _swap_halves(tn) * sin_signed


def _full(shape):
    return pl.BlockSpec(shape, lambda i: (0,) * len(shape))


def _layer_block(shape, layer):
    return pl.BlockSpec((None,) + tuple(shape), lambda i: (layer,) + (0,) * len(shape))


def _qkv_pair(proj, g, gq, gk, mean_mat, cos, sin_signed):
    c = g * PAIR_WIDTH
    q = _head_norm_rope(proj(OFF_Q + c, PAIR_WIDTH), gq, mean_mat, cos, sin_signed)
    q = q * (HEAD_DIM ** -0.5)
    k = _head_norm_rope(proj(OFF_K + c, PAIR_WIDTH), gk, mean_mat, cos, sin_signed)
    v = proj(OFF_VV + c, PAIR_WIDTH)
    return q, k, v


def _proj_kernel(x_ref, gmix_ref, w_ref, gv_ref, gq_ref, gk_ref, cos_ref, sin_ref, mean_ref,
                 u_ref, va_ref, q0_ref, k0_ref, v0_ref, q1_ref, k1_ref, v1_ref,
                 q2_ref, k2_ref, v2_ref, kv0_ref, kv1_ref, kv2_ref, scr_ref):
    rows = TOK_TILE
    qkv_refs = ((q0_ref, k0_ref, v0_ref), (q1_ref, k1_ref, v1_ref), (q2_ref, k2_ref, v2_ref))
    kv_refs = (kv0_ref, kv1_ref, kv2_ref)
    xn = _rms(x_ref[...], gmix_ref[...]).astype(BF16)

    def proj(c0, n):
        return jnp.dot(xn, w_ref[:, c0:c0 + n], preferred_element_type=F32)

    u_ref[...] = jax.nn.gelu(proj(OFF_U, A_WIDTH)).astype(u_ref.dtype)
    va_ref[...] = _rms(jax.nn.gelu(proj(OFF_V, A_WIDTH)), gv_ref[...]).astype(va_ref.dtype)
    cos = cos_ref[...]
    sin_signed = sin_ref[...]
    mean_mat = mean_ref[...]
    n_slabs = PAIR_WIDTH // V7X_LANES

    def store_residue_major(dst_ref, slot, val, dil):
        for j in range(n_slabs):
            scr_ref[slot, j] = val[:, j * V7X_LANES:(j + 1) * V7X_LANES]
        for r in range(dil):
            picked = [scr_ref[slot, j, pl.ds(r, rows // dil, stride=dil), :]
                      for j in range(n_slabs)]
            dst_ref[0, r] = jnp.concatenate(picked, axis=1).astype(dst_ref.dtype)

    slot = 0
    for g, (window, dil) in enumerate(DILATED_PAIRS):
        q, k, v = _qkv_pair(proj, g, gq_ref[...], gk_ref[...], mean_mat, cos, sin_signed)
        keep = min(min(window, SEQ), rows)
        kv_refs[g][0, 0, :, :PAIR_WIDTH] = k[rows - keep:]
        kv_refs[g][0, 0, :, PAIR_WIDTH:] = v[rows - keep:]
        for ref, val in zip(qkv_refs[g], (q, k, v)):
            if dil == 1:
                ref[...] = val.astype(ref.dtype)
            else:
                store_residue_major(ref, slot, val, dil)
                slot += 1


def _proj_prompt(layer, x, p, kv_prev):
    rows = TOK_TILE
    n_tok = x.shape[0]
    nt = TILES_PER_SEQ

    def tok(width):
        return pl.BlockSpec((rows, width), lambda i: (i, 0))

    in_specs = [
        tok(D_MODEL),
        _layer_block((1, D_MODEL), layer),
        _layer_block((D_MODEL, MAIN_COLS), layer),
        _layer_block((1, A_WIDTH), layer),
        _layer_block((1, PAIR_WIDTH), layer),
        _layer_block((1, PAIR_WIDTH), layer),
        pl.BlockSpec((rows, PAIR_WIDTH), lambda i: (i % nt, 0)),
        pl.BlockSpec((rows, PAIR_WIDTH), lambda i: (i % nt, 0)),
        _full((PAIR_WIDTH, PAIR_WIDTH)),
    ]
    out_shape = [jax.ShapeDtypeStruct((n_tok, A_WIDTH), BF16)] * 2
    out_specs = [tok(A_WIDTH)] * 2
    n_strided = 0
    for (window, dil) in DILATED_PAIRS:
        if dil == 1:
            out_shape += [jax.ShapeDtypeStruct((n_tok, PAIR_WIDTH), BF16)] * 3
            out_specs += [tok(PAIR_WIDTH)] * 3
        else:
            n_strided += 3
            out_shape += [jax.ShapeDtypeStruct((BATCH, dil, SEQ // dil, PAIR_WIDTH), BF16)] * 3
            out_specs += [pl.BlockSpec((1, dil, rows // dil, PAIR_WIDTH),
                                       lambda i: (i // nt, 0, i % nt, 0))] * 3
    n_plain_out = len(out_shape)
    for (window, dil) in DILATED_PAIRS:
        keep = min(window, SEQ)
        out_shape.append(jax.ShapeDtypeStruct((DEPTH, BATCH, keep, 2 * PAIR_WIDTH), F32))
        first_tile = (SEQ - max(keep, rows)) // rows
        out_specs.append(pl.BlockSpec(
            (1, 1, min(keep, rows), 2 * PAIR_WIDTH),
            functools.partial(lambda i, ft: (layer, i // nt, jnp.maximum(i % nt - ft, 0), 0),
                              ft=first_tile)))
    args = [x, p["g_mix"], p["w_main"], p["g_v"], p["g_q"], p["g_k"], p["cos_p"], p["sin_p"],
            p["mean_mat"]]
    n_in = len(args)
    aliases = {}
    if kv_prev is not None:
        for j, buf in enumerate(kv_prev):
            in_specs.append(pl.BlockSpec(memory_space=pl.ANY))
            aliases[len(args)] = n_plain_out + j
            args.append(buf)

    def body(*refs):
        _proj_kernel(*refs[:n_in], *refs[n_in + len(aliases):])

    return pl.pallas_call(
        body,
        grid=(n_tok // rows,),
        in_specs=in_specs,
        out_specs=out_specs,
        out_shape=out_shape,
        scratch_shapes=[pltpu.VMEM((n_strided, PAIR_WIDTH // V7X_LANES, rows, V7X_LANES), F32)],
        input_output_aliases=aliases,
        compiler_params=_params(),
        name=f"proj_prompt_l{layer}",
    )(*args)


def _attn_block(q, kk, vv, valid):
    nk = kk.shape[0]
    lane_head = lax.broadcasted_iota(jnp.int32, (1, PAIR_WIDTH), 1) // HEAD_DIM
    head_masks = [lane_head == h for h in range(HEADS_PER_PAIR)]
    qs = jnp.concatenate([jnp.where(hm, q, jnp.zeros_like(q)) for hm in head_masks], axis=0)
    s = lax.dot_general(qs, kk, (((1,), (1,)), ((), ())), preferred_element_type=F32)
    s = jnp.where(valid[None], s.reshape(HEADS_PER_PAIR, QBLK, nk), NEG_INF)
    mx = jnp.max(s, axis=-1, keepdims=True)
    p = jnp.exp(s - mx)
    den = jnp.sum(p, axis=-1, keepdims=True)
    pv = jnp.dot(p.reshape(HEADS_PER_PAIR * QBLK, nk).astype(BF16), vv,
                 preferred_element_type=F32).reshape(HEADS_PER_PAIR, QBLK, PAIR_WIDTH)
    inv_den = 1.0 / den
    lse_h = mx + jnp.log(den)
    o, inv, lse = pv[0], inv_den[0], lse_h[0]
    for h in range(1, HEADS_PER_PAIR):
        o = jnp.where(head_masks[h], pv[h], o)
        inv = jnp.where(head_masks[h], inv_den[h], inv)
        lse = jnp.where(head_masks[h], lse_h[h], lse)
    return o * inv, jnp.broadcast_to(lse, (QBLK, PAIR_WIDTH))


def _attn_kernel(q0_ref, k0_ref, v0_ref, q1_ref, k1_ref, v1_ref, q2_ref, k2_ref, v2_ref,
                 out_ref, o_scr, l_scr):
    qi = lax.broadcasted_iota(jnp.int32, (QBLK, QBLK), 0)
    ki = lax.broadcasted_iota(jnp.int32, (QBLK, QBLK), 1)
    causal = ki <= qi
    qi2 = lax.broadcasted_iota(jnp.int32, (QBLK, 2 * QBLK), 0)
    ki2 = lax.broadcasted_iota(jnp.int32, (QBLK, 2 * QBLK), 1)
    dist = qi2 - ki2 + QBLK
    band = (dist >= 0) & (dist <= TAPS)
    n_slabs = PAIR_WIDTH // V7X_LANES

    def put(scr, g, rows, val):
        for j in range(n_slabs):
            scr[g, j, rows, :] = val[:, j * V7X_LANES:(j + 1) * V7X_LANES]

    def get(scr, g, rows):
        return jnp.concatenate([scr[g, j, rows, :] for j in range(n_slabs)], axis=1)

    for g, refs in enumerate(((q0_ref, k0_ref, v0_ref), (q1_ref, k1_ref, v1_ref),
                              (q2_ref, k2_ref, v2_ref))):
        dil = DILATED_PAIRS[g][1]
        n_blocks = (SEQ // dil) // QBLK

        def rows_of(ref, r, start, dil=dil):
            if dil == 1:
                return ref[pl.ds(start, QBLK), :]
            return ref[0, r, pl.ds(start, QBLK), :]

        def step(idx, carry, g=g, dil=dil, n_blocks=n_blocks, refs=refs, rows_of=rows_of):
            q_ref, k_ref, v_ref = refs
            r, i = idx // n_blocks, idx % n_blocks
            cur = pl.multiple_of(i * QBLK, QBLK)
            q = rows_of(q_ref, r, cur)
            if n_blocks == 1:
                o, lse = _attn_block(q, rows_of(k_ref, r, cur), rows_of(v_ref, r, cur), causal)
            else:
                prev = pl.multiple_of(jnp.maximum(i - 1, 0) * QBLK, QBLK)
                kk = jnp.concatenate([rows_of(k_ref, r, prev), rows_of(k_ref, r, cur)], axis=0)
                vv = jnp.concatenate([rows_of(v_ref, r, prev), rows_of(v_ref, r, cur)], axis=0)
                first_key = jnp.where(i > 0, 0, QBLK)
                o, lse = _attn_block(q, kk, vv, band & (ki2 >= first_key))
            if dil == 1:
                rows = pl.ds(cur, QBLK)
            else:
                rows = pl.ds(r + i * (QBLK * dil), QBLK, stride=dil)
            put(o_scr, g, rows, o)
            put(l_scr, g, rows, lse)
            return carry

        lax.fori_loop(0, dil * n_blocks, step, 0, unroll=ATTN_UNROLL)

    step_rows = 256
    for t0 in range(0, SEQ, step_rows):
        rows = slice(t0, t0 + step_rows)
        l0, l1, l2 = get(l_scr, 0, rows), get(l_scr, 1, rows), get(l_scr, 2, rows)
        m = jnp.maximum(jnp.maximum(l0, l1), l2)
        e0, e1, e2 = jnp.exp(l0 - m), jnp.exp(l1 - m), jnp.exp(l2 - m)
        den = e0 + e1 + e2
        merged = ((e0 / den) * get(o_scr, 0, rows) + (e1 / den) * get(o_scr, 1, rows)
                  + (e2 / den) * get(o_scr, 2, rows))
        out_ref[rows, :] = merged.astype(out_ref.dtype)


def _attn_prompt(layer, qkv):
    n_tok = qkv[0].shape[0]
    in_specs = []
    for (window, dil) in DILATED_PAIRS:
        if dil == 1:
            in_specs += [pl.BlockSpec((SEQ, PAIR_WIDTH), lambda b: (b, 0))] * 3
        else:
            in_specs += [pl.BlockSpec((1, dil, SEQ // dil, PAIR_WIDTH),
                                      lambda b: (b, 0, 0, 0))] * 3
    scr = pltpu.VMEM((N_PAIRS, PAIR_WIDTH // V7X_LANES, SEQ, V7X_LANES), F32)
    return pl.pallas_call(
        _attn_kernel,
        grid=(BATCH,),
        in_specs=in_specs,
        out_specs=pl.BlockSpec((SEQ, PAIR_WIDTH), lambda b: (b, 0)),
        out_shape=jax.ShapeDtypeStruct((n_tok, PAIR_WIDTH), BF16),
        scratch_shapes=[scr, scr],
        compiler_params=_params(),
        name=f"attn_prompt_l{layer}",
    )(*qkv)


def _gated_merge(x, a_out, b_out, ga, gb, wa_ref, wb_ref, wo_ref):
    ap = jnp.dot(a_out.astype(BF16), wa_ref[...], preferred_element_type=F32)
    bp = jnp.dot(b_out.astype(BF16), wb_ref[...], preferred_element_type=F32)
    merged = ga * ap + gb * bp
    return x + jnp.dot(merged.astype(BF16), wo_ref[...], preferred_element_type=F32)


def _gates(xn, wg_ref):
    ga = jax.nn.sigmoid(jnp.dot(xn, wg_ref[:, :D_MODEL], preferred_element_type=F32))
    gb = jax.nn.sigmoid(jnp.dot(xn, wg_ref[:, D_MODEL:], preferred_element_type=F32))
    return ga, gb


def _mix_kernel(x_ref, u_ref, va_ref, bo_ref, gmix_ref, wg_ref, ws_ref, bias_ref,
                wa_ref, wb_ref, wo_ref, out_ref, a_scr):
    ti = lax.broadcasted_iota(jnp.int32, (CHUNK, CHUNK), 0)
    si = lax.broadcasted_iota(jnp.int32, (CHUNK, CHUNK), 1)
    tril = si <= ti
    w_tril = [jnp.where(tril, ws_ref[g], 0.0).astype(BF16) for g in range(A_GROUPS)]
    low_group = lax.broadcasted_iota(jnp.int32, (1, V7X_LANES), 1) < A_GROUP_DIM
    bias = bias_ref[...]
    groups_per_slab = V7X_LANES // A_GROUP_DIM
    for c in range(TOK_TILE // CHUNK):
        rows = slice(c * CHUNK, (c + 1) * CHUNK)
        zs = []
        for j in range(A_WIDTH // V7X_LANES):
            slab = va_ref[rows, j * V7X_LANES:(j + 1) * V7X_LANES]
            z_lo = jnp.dot(w_tril[groups_per_slab * j], slab, preferred_element_type=F32)
            z_hi = jnp.dot(w_tril[groups_per_slab * j + 1], slab, preferred_element_type=F32)
            zs.append(jnp.where(low_group, z_lo, z_hi))
        z = jnp.concatenate(zs, axis=1) + bias
        a_scr[rows, :] = (u_ref[rows, :].astype(F32) * z).astype(BF16)
    x = x_ref[...]
    ga, gb = _gates(_rms(x, gmix_ref[...]).astype(BF16), wg_ref)
    out_ref[...] = _gated_merge(x, a_scr[...], bo_ref[...], ga, gb, wa_ref, wb_ref, wo_ref)


def _mix_prompt(layer, x, u, va, b_out, p):
    rows = TOK_TILE
    n_tok = x.shape[0]

    def tok(width):
        return pl.BlockSpec((rows, width), lambda i: (i, 0))

    in_specs = [
        tok(D_MODEL), tok(A_WIDTH), tok(A_WIDTH), tok(PAIR_WIDTH),
        _layer_block((1, D_MODEL), layer),
        _layer_block((D_MODEL, 2 * D_MODEL), layer),
        _layer_block((A_GROUPS, CHUNK, CHUNK), layer),
        _layer_block((CHUNK, A_WIDTH), layer),
        _layer_block((A_WIDTH, D_MODEL), layer),
        _layer_block((PAIR_WIDTH, D_MODEL), layer),
        _layer_block((D_MODEL, D_MODEL), layer),
    ]
    return pl.pallas_call(
        _mix_kernel,
        grid=(n_tok // rows,),
        in_specs=in_specs,
        out_specs=tok(D_MODEL),
        out_shape=jax.ShapeDtypeStruct((n_tok, D_MODEL), F32),
        scratch_shapes=[pltpu.VMEM((rows, A_WIDTH), BF16)],
        compiler_params=_params(),
        name=f"mix_prompt_l{layer}",
    )(x, u, va, b_out, p["g_mix"], p["w_gates"], p["w_s"], p["bias_full"],
      p["w_a"], p["w_b"], p["w_o"])


def _ffn_kernel(x_ref, halo_ref, g_ref, wup_ref, cw_ref, cb_ref, wdown_ref,
                out_ref, conv_ref, xn_scr, up_scr, h_scr):
    rows = TOK_TILE
    tile = pl.program_id(0) % TILES_PER_SEQ
    x = x_ref[...]
    g = g_ref[...]
    halo_on = jnp.where(tile == 0, 0.0, 1.0)
    xn_scr[:HALO, :] = (_rms(halo_ref[...], g) * halo_on).astype(BF16)
    xn_scr[HALO:, :] = _rms(x, g).astype(BF16)
    xn = xn_scr[...]
    n_slabs = FF_CHUNK // V7X_LANES

    def up_and_conv(c0, slot):
        cols = slice(c0, c0 + FF_CHUNK)
        up = jnp.dot(xn, wup_ref[:, cols], preferred_element_type=F32)
        conv_ref[0, :, cols] = up[HALO + rows - (CONV_W - 1):]
        for s in range(n_slabs):
            up_scr[slot, s] = up[:, s * V7X_LANES:(s + 1) * V7X_LANES]
        parts = []
        for s in range(n_slabs):
            cs = slice(c0 + s * V7X_LANES, c0 + (s + 1) * V7X_LANES)
            acc = cb_ref[:, cs]
            for j in range(CONV_W - 1):
                lo = HALO - (CONV_W - 1) + j
                acc = acc + cw_ref[j:j + 1, cs] * up_scr[slot, s, lo:lo + rows, :]
            parts.append(acc + cw_ref[CONV_W - 1:CONV_W, cs]
                         * up[HALO:, s * V7X_LANES:(s + 1) * V7X_LANES])
        return jnp.concatenate(parts, axis=1)

    for j in range(N_FF_CHUNKS):
        c_gate = up_and_conv(j * FF_CHUNK, 0)
        c_val = up_and_conv(D_FF + j * FF_CHUNK, 1)
        h_scr[:, j * FF_CHUNK:(j + 1) * FF_CHUNK] = (jax.nn.silu(c_gate) * c_val).astype(BF16)
    out_ref[...] = x + jnp.dot(h_scr[...], wdown_ref[...], preferred_element_type=F32)


def _ffn_prompt(layer, x, p):
    rows = TOK_TILE
    n_tok = x.shape[0]
    nt = TILES_PER_SEQ
    halo_per_tile = rows // HALO
    in_specs = [
        pl.BlockSpec((rows, D_MODEL), lambda i: (i, 0)),
        pl.BlockSpec((HALO, D_MODEL), lambda i: (jnp.maximum(i * halo_per_tile - 1, 0), 0)),
        _layer_block((1, D_MODEL), layer),
        _layer_block((D_MODEL, 2 * D_FF), layer),
        _layer_block((CONV_W, 2 * D_FF), layer),
        _layer_block((1, 2 * D_FF), layer),
        _layer_block((D_FF, D_MODEL), layer),
    ]
    return pl.pallas_call(
        _ffn_kernel,
        grid=(n_tok // rows,),
        in_specs=in_specs,
        out_specs=[pl.BlockSpec((rows, D_MODEL), lambda i: (i, 0)),
                   pl.BlockSpec((1, CONV_W - 1, 2 * D_FF), lambda i: (i // nt, 0, 0))],
        out_shape=[jax.ShapeDtypeStruct((n_tok, D_MODEL), F32),
                   jax.ShapeDtypeStruct((BATCH, CONV_W - 1, 2 * D_FF), F32)],
        scratch_shapes=[pltpu.VMEM((HALO + rows, D_MODEL), BF16),
                        pltpu.VMEM((2, FF_CHUNK // V7X_LANES, HALO + rows, V7X_LANES), F32),
                        pltpu.VMEM((rows, D_FF), BF16)],
        compiler_params=_params(),
        name=f"ffn_prompt_l{layer}",
    )(x, x, p["g_ffn"], p["w_up"], p["conv_w"], p["conv_b"], p["w_down"])


def _sample_proj_kernel(x_ref, gmix_ref, w_ref, wg_ref, gv_ref, gq_ref, gk_ref, cos_ref, sin_ref,
                        mean_ref, u_ref, va_ref, q_ref, kv_ref, ga_ref, gb_ref):
    xn = _rms(x_ref[...], gmix_ref[...]).astype(BF16)

    def proj(c0, n):
        return jnp.dot(xn, w_ref[:, c0:c0 + n], preferred_element_type=F32)

    u_ref[...] = jax.nn.gelu(proj(OFF_U, A_WIDTH))
    va_ref[...] = _rms(jax.nn.gelu(proj(OFF_V, A_WIDTH)), gv_ref[...])
    ga_ref[...], gb_ref[...] = _gates(xn, wg_ref)
    for g in range(N_PAIRS):
        q, k, v = _qkv_pair(proj, g, gq_ref[...], gk_ref[...], mean_ref[...], cos_ref[...],
                            sin_ref[...])
        q_ref[:, g * PAIR_WIDTH:(g + 1) * PAIR_WIDTH] = q
        kv_ref[:, 2 * g * PAIR_WIDTH:(2 * g + 1) * PAIR_WIDTH] = k
        kv_ref[:, (2 * g + 1) * PAIR_WIDTH:(2 * g + 2) * PAIR_WIDTH] = v


def _proj_sample(layer, x, p):
    rows = x.shape[0]
    in_specs = [
        _full((rows, D_MODEL)),
        _layer_block((1, D_MODEL), layer),
        _layer_block((D_MODEL, MAIN_COLS), layer),
        _layer_block((D_MODEL, 2 * D_MODEL), layer),
        _layer_block((1, A_WIDTH), layer),
        _layer_block((1, PAIR_WIDTH), layer),
        _layer_block((1, PAIR_WIDTH), layer),
        _full((rows, PAIR_WIDTH)),
        _full((rows, PAIR_WIDTH)),
        _full((PAIR_WIDTH, PAIR_WIDTH)),
    ]
    widths = [A_WIDTH, A_WIDTH, B_WIDTH, 2 * B_WIDTH, D_MODEL, D_MODEL]
    return pl.pallas_call(
        _sample_proj_kernel,
        grid=(1,),
        in_specs=in_specs,
        out_specs=[_full((rows, w)) for w in widths],
        out_shape=[jax.ShapeDtypeStruct((rows, w), F32) for w in widths],
        compiler_params=_params(),
        name=f"proj_sample_l{layer}",
    )(x, p["g_mix"], p["w_main"], p["w_gates"], p["g_v"], p["g_q"], p["g_k"],
      p["cos_s"], p["sin_s"], p["mean_mat"])


def _sample_attn_kernel(qt_ref, kvt_ref, c0_ref, c1_ref, c2_ref, out_ref, o_scr, l_scr):
    for g, c_ref in enumerate((c0_ref, c1_ref, c2_ref)):
        window, dil = DILATED_PAIRS[g]
        pos = lax.broadcasted_iota(jnp.int32, (1, window), 1)
        tap = (pos % dil) == 0
        for b in range(SAMPLE_GROUP):
            col = slice(b, b + 1)
            for h in range(HEADS_PER_PAIR):
                feat = slice(g * PAIR_WIDTH + h * HEAD_DIM, g * PAIR_WIDTH + (h + 1) * HEAD_DIM)
                kfeat = slice(2 * g * PAIR_WIDTH + h * HEAD_DIM,
                              2 * g * PAIR_WIDTH + (h + 1) * HEAD_DIM)
                vfeat = slice((2 * g + 1) * PAIR_WIDTH + h * HEAD_DIM,
                              (2 * g + 1) * PAIR_WIDTH + (h + 1) * HEAD_DIM)
                q = qt_ref[0, feat, col]
                k_new = kvt_ref[0, kfeat, col]
                v_new = kvt_ref[0, vfeat, col]
                s = jnp.sum(c_ref[0, b, 0, h] * q, axis=0, keepdims=True)
                s = jnp.where(tap, s, NEG_INF)
                s_new = jnp.sum(k_new * q, axis=0, keepdims=True)
                mx = jnp.maximum(jnp.max(s, axis=1, keepdims=True), s_new)
                pr = jnp.exp(s - mx)
                p_new = jnp.exp(s_new - mx)
                den = jnp.sum(pr, axis=1, keepdims=True) + p_new
                o = (jnp.sum(c_ref[0, b, 1, h] * pr, axis=1, keepdims=True) + p_new * v_new) / den
                out_rows = slice(h * HEAD_DIM, (h + 1) * HEAD_DIM)
                o_scr[g, out_rows, col] = o
                l_scr[g, out_rows, col] = jnp.broadcast_to(mx + jnp.log(den), (HEAD_DIM, 1))
    l0, l1, l2 = l_scr[0], l_scr[1], l_scr[2]
    m = jnp.maximum(jnp.maximum(l0, l1), l2)
    e0, e1, e2 = jnp.exp(l0 - m), jnp.exp(l1 - m), jnp.exp(l2 - m)
    den = e0 + e1 + e2
    out_ref[0] = (e0 / den) * o_scr[0] + (e1 / den) * o_scr[1] + (e2 / den) * o_scr[2]


def _attn_sample(layer, qt, kvt, caches):
    n = SAMPLE_GROUP
    groups = qt.shape[0]
    in_specs = [pl.BlockSpec((1, B_WIDTH, n), lambda i: (i, 0, 0)),
                pl.BlockSpec((1, 2 * B_WIDTH, n), lambda i: (i, 0, 0))]
    for (window, dil) in DILATED_PAIRS:
        in_specs.append(pl.BlockSpec((1, n, 2, HEADS_PER_PAIR, HEAD_DIM, window),
                                     lambda i: (layer, i, 0, 0, 0, 0)))
    scr = pltpu.VMEM((N_PAIRS, PAIR_WIDTH, n), F32)
    return pl.pallas_call(
        _sample_attn_kernel,
        grid=(groups,),
        in_specs=in_specs,
        out_specs=pl.BlockSpec((1, PAIR_WIDTH, n), lambda i: (i, 0, 0)),
        out_shape=jax.ShapeDtypeStruct((groups, PAIR_WIDTH, n), F32),
        scratch_shapes=[scr, scr],
        compiler_params=_params(),
        name=f"attn_sample_l{layer}",
    )(qt, kvt, *caches)


def _sample_mix_kernel(x_ref, u_ref, va_ref, bo_ref, ga_ref, gb_ref, wdiag_ref, bias0_ref,
                       wa_ref, wb_ref, wo_ref, out_ref):
    a_out = u_ref[...] * (wdiag_ref[...] * va_ref[...] + bias0_ref[...])
    out_ref[...] = _gated_merge(x_ref[...], a_out, bo_ref[...], ga_ref[...], gb_ref[...],
                                wa_ref, wb_ref, wo_ref)


def _mix_sample(layer, x, u, va, b_out, ga, gb, p):
    rows = x.shape[0]
    in_specs = [_full((rows, D_MODEL)), _full((rows, A_WIDTH)), _full((rows, A_WIDTH)),
                _full((rows, PAIR_WIDTH)), _full((rows, D_MODEL)), _full((rows, D_MODEL)),
                _layer_block((1, A_WIDTH), layer), _layer_block((1, A_WIDTH), layer),
                _layer_block((A_WIDTH, D_MODEL), layer),
                _layer_block((PAIR_WIDTH, D_MODEL), layer),
                _layer_block((D_MODEL, D_MODEL), layer)]
    return pl.pallas_call(
        _sample_mix_kernel,
        grid=(1,),
        in_specs=in_specs,
        out_specs=_full((rows, D_MODEL)),
        out_shape=jax.ShapeDtypeStruct((rows, D_MODEL), F32),
        compiler_params=_params(),
        name=f"mix_sample_l{layer}",
    )(x, u, va, b_out, ga, gb, p["wdiag"], p["bias0"], p["w_a"], p["w_b"], p["w_o"])


def _sample_ffn_kernel(x_ref, g_ref, wg_ref, wv_ref, cwg_ref, cwv_ref, cbg_ref, cbv_ref,
                       h0g_ref, h0v_ref, h1g_ref, h1v_ref, wdown_ref,
                       out_ref, upg_ref, upv_ref):
    j = pl.program_id(0)
    x = x_ref[...]
    xn = _rms(x, g_ref[...]).astype(BF16)

    def up_and_conv(w_ref, cw_ref, cb_ref, h0_ref, h1_ref, up_ref):
        up = jnp.dot(xn, w_ref[...], preferred_element_type=F32)
        up_ref[...] = up
        return (cb_ref[...] + cw_ref[0:1, :] * h0_ref[...] + cw_ref[1:2, :] * h1_ref[...]
                + cw_ref[2:3, :] * up)

    c_gate = up_and_conv(wg_ref, cwg_ref, cbg_ref, h0g_ref, h1g_ref, upg_ref)
    c_val = up_and_conv(wv_ref, cwv_ref, cbv_ref, h0v_ref, h1v_ref, upv_ref)
    h = (jax.nn.silu(c_gate) * c_val).astype(BF16)
    part = jnp.dot(h, wdown_ref[...], preferred_element_type=F32)

    @pl.when(j == 0)
    def _():
        out_ref[...] = x + part

    @pl.when(j > 0)
    def _():
        out_ref[...] += part


def _ffn_sample(layer, x, p, state_conv):
    rows = x.shape[0]
    nch = N_FF_CHUNKS
    fc = FF_CHUNK

    def cols(nrows, blk):
        return pl.BlockSpec((None, nrows, fc), lambda j: (layer, 0, blk * nch + j))

    state_flat = state_conv.reshape(DEPTH, rows, (CONV_W - 1) * 2 * D_FF)
    hist = [cols(rows, 2 * row + half) for row in range(CONV_W - 1) for half in range(2)]
    in_specs = [
        _full((rows, D_MODEL)),
        _layer_block((1, D_MODEL), layer),
        cols(D_MODEL, 0), cols(D_MODEL, 1),
        cols(CONV_W, 0), cols(CONV_W, 1),
        cols(1, 0), cols(1, 1),
        *hist,
        pl.BlockSpec((None, fc, D_MODEL), lambda j: (layer, j, 0)),
    ]
    up_spec = pl.BlockSpec((rows, fc), lambda j: (0, j))
    up_shape = jax.ShapeDtypeStruct((rows, D_FF), F32)
    return pl.pallas_call(
        _sample_ffn_kernel,
        grid=(nch,),
        in_specs=in_specs,
        out_specs=[_full((rows, D_MODEL)), up_spec, up_spec],
        out_shape=[jax.ShapeDtypeStruct((rows, D_MODEL), F32), up_shape, up_shape],
        compiler_params=_params(),
        name=f"ffn_sample_l{layer}",
    )(x, p["g_ffn"], p["w_up"], p["w_up"], p["conv_w"], p["conv_w"], p["conv_b"], p["conv_b"],
      state_flat, state_flat, state_flat, state_flat, p["w_down"])


def _sample_layer(layer, xs, p, caches_t, state_conv):
    n = SAMPLE_GROUP
    rows = xs.shape[0]
    u, va, q, kv, ga, gb = _proj_sample(layer, xs, p)

    def columns(a):
        return a.reshape(rows // n, n, a.shape[1]).swapaxes(1, 2)

    bt = _attn_sample(layer, columns(q), columns(kv), caches_t)
    b_out = bt.swapaxes(1, 2).reshape(rows, PAIR_WIDTH)
    xs = _mix_sample(layer, xs, u, va, b_out, ga, gb, p)
    xs, upg, upv = _ffn_sample(layer, xs, p, state_conv)
    up_row = jnp.concatenate([upg, upv], axis=-1)[:, None, :]
    new_conv = jnp.concatenate([state_conv[layer][:, 1:], up_row], axis=1)
    return xs, kv, va, new_conv


def _rope_tables(pos):
    half = HEAD_DIM // 2
    inv_freq = ROPE_THETA ** (-jnp.arange(half, dtype=F32) / half)
    ang = pos.astype(F32)[:, None] * inv_freq[None, :]
    cos, sin = jnp.cos(ang), jnp.sin(ang)
    cos_full = jnp.tile(jnp.concatenate([cos, cos], axis=1), (1, HEADS_PER_PAIR))
    sin_signed = jnp.tile(jnp.concatenate([-sin, sin], axis=1), (1, HEADS_PER_PAIR))
    return cos_full, sin_signed


def _prepare_params(g_mix, w_in, g_v, w_s, b_s, g_q, g_k, w_a_proj, w_b_proj, w_o,
                    g_ffn, w_up, conv_w, conv_b, w_down):
    head_of = jnp.arange(PAIR_WIDTH) // HEAD_DIM
    same_head = head_of[:, None] == head_of[None, :]
    bias_full = jnp.repeat(jnp.swapaxes(b_s, 1, 2), A_GROUP_DIM, axis=2)
    cos_p, sin_p = _rope_tables(jnp.arange(SEQ))
    cos_s, sin_s = _rope_tables(jnp.full((DEC_BATCH,), PAST_LEN))
    return dict(
        g_mix=g_mix[:, None, :], g_v=g_v[:, None, :], g_ffn=g_ffn[:, None, :],
        g_q=jnp.tile(g_q, (1, HEADS_PER_PAIR))[:, None, :],
        g_k=jnp.tile(g_k, (1, HEADS_PER_PAIR))[:, None, :],
        w_main=w_in[:, :, :MAIN_COLS].astype(BF16),
        w_gates=w_in[:, :, OFF_GATES:].astype(BF16),
        w_a=w_a_proj.astype(BF16), w_b=w_b_proj.astype(BF16), w_o=w_o.astype(BF16),
        w_up=w_up.astype(BF16), w_down=w_down.astype(BF16),
        w_s=w_s, bias_full=bias_full, bias0=bias_full[:, 0:1, :],
        wdiag=jnp.repeat(w_s[:, :, 0, 0], A_GROUP_DIM, axis=1)[:, None, :],
        conv_w=conv_w, conv_b=conv_b[:, None, :],
        mean_mat=jnp.where(same_head, 1.0 / HEAD_DIM, 0.0).astype(BF16),
        cos_p=cos_p, sin_p=sin_p, cos_s=cos_s, sin_s=sin_s,
    )


def kernel(x_prompt, x_sample, cache_kv_w128, cache_kv_w512, cache_kv_w2048, state_conv,
           g_mix, w_in, g_v, w_s, b_s, g_q, g_k, w_a_proj, w_b_proj, w_o,
           g_ffn, w_up, conv_w, conv_b, w_down):
    assert x_prompt.shape == (BATCH, SEQ, D_MODEL) and x_sample.shape == (DEC_BATCH, 1, D_MODEL)
    p = _prepare_params(g_mix, w_in, g_v, w_s, b_s, g_q, g_k, w_a_proj, w_b_proj, w_o,
                        g_ffn, w_up, conv_w, conv_b, w_down)
    caches_t = []
    for cache, (window, dil) in zip((cache_kv_w128, cache_kv_w512, cache_kv_w2048), DILATED_PAIRS):
        assert cache.shape[2] == window
        caches_t.append(jnp.transpose(cache, (0, 1, 3, 4, 5, 2)))

    xp = x_prompt.reshape(BATCH * SEQ, D_MODEL)
    xs = x_sample.reshape(DEC_BATCH, D_MODEL)
    kv_prompt = None
    conv_p, conv_s, kv_s, va_s = [], [], [], []
    for l in range(DEPTH):
        outs = _proj_prompt(l, xp, p, kv_prompt)
        u, va = outs[0], outs[1]
        qkv = outs[2:11]
        kv_prompt = outs[11:14]
        b_out = _attn_prompt(l, qkv)
        xp = _mix_prompt(l, xp, u, va, b_out, p)
        xp, nconv = _ffn_prompt(l, xp, p)
        conv_p.append(nconv)

        xs, skv, sva, sconv = _sample_layer(l, xs, p, caches_t, state_conv)
        kv_s.append(skv)
        va_s.append(sva)
        conv_s.append(sconv)

    def kv_shape(a, rows):
        return a.reshape(DEPTH, -1, rows, 2, HEADS_PER_PAIR, HEAD_DIM)

    kv_s = jnp.stack(kv_s, axis=0)
    kv_s = [kv_s[:, :, 2 * g * PAIR_WIDTH:(2 * g + 2) * PAIR_WIDTH] for g in range(N_PAIRS)]
    return (
        xp.reshape(BATCH, SEQ, D_MODEL),
        xs.reshape(DEC_BATCH, 1, D_MODEL),
        kv_shape(kv_prompt[0], min(DILATED_PAIRS[0][0], SEQ)),
        kv_shape(kv_prompt[1], min(DILATED_PAIRS[1][0], SEQ)),
        kv_shape(kv_prompt[2], min(DILATED_PAIRS[2][0], SEQ)),
        jnp.stack(conv_p, axis=0),
        kv_shape(kv_s[0], 1),
        kv_shape(kv_s[1], 1),
        kv_shape(kv_s[2], 1),
        jnp.stack(conv_s, axis=0),
        jnp.stack(va_s, axis=0).reshape(DEPTH, DEC_BATCH, 1, A_WIDTH),
    )
```

```python
import functools

import jax
import jax.numpy as jnp
from jax import lax
from jax.experimental import pallas as pl
from jax.experimental.pallas import tpu as pltpu

D_MODEL = 1024
BATCH = 16
SEQ = 2048
DEPTH = 2
DEC_BATCH = 32
PAST_LEN = 16384
CHUNK = 128
A_GROUPS = 8
A_GROUP_DIM = 64
A_WIDTH = A_GROUPS * A_GROUP_DIM
HEAD_DIM = 64
HEADS_PER_PAIR = 4
DILATED_PAIRS = ((128, 1), (512, 4), (2048, 16))
N_PAIRS = len(DILATED_PAIRS)
TAPS = DILATED_PAIRS[0][0] // DILATED_PAIRS[0][1]
PAIR_WIDTH = HEADS_PER_PAIR * HEAD_DIM
B_WIDTH = N_PAIRS * PAIR_WIDTH
QBLK = 128
D_FF = 2816
CONV_W = 3
ROPE_THETA = 10000.0
EPS = 1e-6
NEG_INF = -1e30
LOG2E = 1.4426950408889634
LN2 = 0.6931471805599453
OFF_U, OFF_V = 0, A_WIDTH
OFF_Q = 2 * A_WIDTH
OFF_K = OFF_Q + B_WIDTH
OFF_VV = OFF_K + B_WIDTH
OFF_GATES = OFF_VV + B_WIDTH
MAIN_COLS = OFF_GATES

V7X_LANES = 128
V7X_VMEM_BYTES = 64 * 1024 * 1024
VMEM_LIMIT = V7X_VMEM_BYTES - 8 * 1024 * 1024

TOK_TILE = 512
TILES_PER_SEQ = SEQ // TOK_TILE
PROJ_TILE = 512
HALO = 16
FF_CHUNK = 256
N_FF_CHUNKS = D_FF // FF_CHUNK
ATTN_UNROLL = 8
SAMPLE_GROUP = 2

F32 = jnp.float32
BF16 = jnp.bfloat16

assert all(w // d == TAPS for (w, d) in DILATED_PAIRS) and TAPS >= QBLK - 1


def _params(**kw):
    return pltpu.CompilerParams(dimension_semantics=("arbitrary",),
                                vmem_limit_bytes=VMEM_LIMIT, **kw)


def _rms(x, g):
    return x * lax.rsqrt(jnp.mean(x * x, axis=-1, keepdims=True) + EPS) * g


def _swap_halves(t):
    lane = lax.broadcasted_iota(jnp.int32, (1, V7X_LANES), 1)
    first_half = (lane % HEAD_DIM) < (HEAD_DIM // 2)
    slabs = []
    for j in range(t.shape[1] // V7X_LANES):
        s = t[:, j * V7X_LANES:(j + 1) * V7X_LANES]
        up = pltpu.roll(s, V7X_LANES - HEAD_DIM // 2, 1)
        down = pltpu.roll(s, HEAD_DIM // 2, 1)
        slabs.append(jnp.where(first_half, up, down))
    return jnp.concatenate(slabs, axis=1)


def _head_norm_rope(t, gain, mean_mat, cos, sin_signed):
    ms = jnp.dot((t * t).astype(BF16), mean_mat, preferred_element_type=F32)
    tn = t * lax.rsqrt(ms + EPS) * gain
    return tn * cos + _swap_halves(tn) * sin_signed


def _full(shape):
    return pl.BlockSpec(shape, lambda i: (0,) * len(shape))


def _layer_block(shape, layer):
    return pl.BlockSpec((None,) + tuple(shape), lambda i: (layer,) + (0,) * len(shape))


def _qkv_pair(proj, g, gq, gk, mean_mat, cos, sin_signed, q_scale=HEAD_DIM ** -0.5):
    c = g * PAIR_WIDTH
    q = _head_norm_rope(proj(OFF_Q + c, PAIR_WIDTH), gq, mean_mat, cos, sin_signed)
    q = q * q_scale
    k = _head_norm_rope(proj(OFF_K + c, PAIR_WIDTH), gk, mean_mat, cos, sin_signed)
    v = proj(OFF_VV + c, PAIR_WIDTH)
    return q, k, v


def _proj_kernel(x_ref, gmix_ref, w_ref, gv_ref, gq_ref, gk_ref, cos_ref, sin_ref, mean_ref,
                 u_ref, va_ref, q0_ref, k0_ref, v0_ref, q1_ref, k1_ref, v1_ref,
                 q2_ref, k2_ref, v2_ref, kv0_ref, kv1_ref, kv2_ref, scr_ref):
    rows = PROJ_TILE
    qkv_refs = ((q0_ref, k0_ref, v0_ref), (q1_ref, k1_ref, v1_ref), (q2_ref, k2_ref, v2_ref))
    kv_refs = (kv0_ref, kv1_ref, kv2_ref)
    xn = _rms(x_ref[...], gmix_ref[...]).astype(BF16)

    def proj(c0, n):
        return jnp.dot(xn, w_ref[:, c0:c0 + n], preferred_element_type=F32)

    cos = cos_ref[...]
    sin_signed = sin_ref[...]
    mean_mat = mean_ref[...]
    n_slabs = PAIR_WIDTH // V7X_LANES

    def store_residue_major(dst_ref, slot, val, dil):
        for j in range(n_slabs):
            scr_ref[slot, j] = val[:, j * V7X_LANES:(j + 1) * V7X_LANES]
        for r in range(dil):
            picked = [scr_ref[slot, j, pl.ds(r, rows // dil, stride=dil), :]
                      for j in range(n_slabs)]
            dst_ref[0, r] = jnp.concatenate(picked, axis=1).astype(dst_ref.dtype)

    slot = 0
    for g, (window, dil) in reversed(list(enumerate(DILATED_PAIRS))):
        q, k, v = _qkv_pair(proj, g, gq_ref[...], gk_ref[...], mean_mat, cos, sin_signed,
                            q_scale=HEAD_DIM ** -0.5 * LOG2E)
        keep = min(min(window, SEQ), rows)
        kv_refs[g][0, 0, :, :PAIR_WIDTH] = k[rows - keep:]
        kv_refs[g][0, 0, :, PAIR_WIDTH:] = v[rows - keep:]
        for ref, val in zip(qkv_refs[g], (q, k, v)):
            if dil == 1:
                ref[...] = val.astype(ref.dtype)
            else:
                store_residue_major(ref, slot, val, dil)
                slot += 1
    va_ref[...] = _rms(jax.nn.gelu(proj(OFF_V, A_WIDTH)), gv_ref[...]).astype(va_ref.dtype)
    u_ref[...] = jax.nn.gelu(proj(OFF_U, A_WIDTH)).astype(u_ref.dtype)


def _proj_prompt(layer, x, p, kv_prev):
    rows = PROJ_TILE
    n_tok = x.shape[0]
    nt = SEQ // rows

    def tok(width):
        return pl.BlockSpec((rows, width), lambda i: (i, 0))

    in_specs = [
        tok(D_MODEL),
        _layer_block((1, D_MODEL), layer),
        _layer_block((D_MODEL, MAIN_COLS), layer),
        _layer_block((1, A_WIDTH), layer),
        _layer_block((1, PAIR_WIDTH), layer),
        _layer_block((1, PAIR_WIDTH), layer),
        pl.BlockSpec((rows, PAIR_WIDTH), lambda i: (i % nt, 0)),
        pl.BlockSpec((rows, PAIR_WIDTH), lambda i: (i % nt, 0)),
        _full((PAIR_WIDTH, PAIR_WIDTH)),
    ]
    out_shape = [jax.ShapeDtypeStruct((n_tok, A_WIDTH), BF16)] * 2
    out_specs = [tok(A_WIDTH)] * 2
    n_strided = 0
    for (window, dil) in DILATED_PAIRS:
        if dil == 1:
            out_shape += [jax.ShapeDtypeStruct((n_tok, PAIR_WIDTH), BF16)] * 3
            out_specs += [tok(PAIR_WIDTH)] * 3
        else:
            n_strided += 3
            out_shape += [jax.ShapeDtypeStruct((BATCH, dil, SEQ // dil, PAIR_WIDTH), BF16)] * 3
            out_specs += [pl.BlockSpec((1, dil, rows // dil, PAIR_WIDTH),
                                       lambda i: (i // nt, 0, i % nt, 0))] * 3
    n_plain_out = len(out_shape)
    for (window, dil) in DILATED_PAIRS:
        keep = min(window, SEQ)
        out_shape.append(jax.ShapeDtypeStruct((DEPTH, BATCH, keep, 2 * PAIR_WIDTH), F32))
        first_tile = (SEQ - max(keep, rows)) // rows
        out_specs.append(pl.BlockSpec(
            (1, 1, min(keep, rows), 2 * PAIR_WIDTH),
            functools.partial(lambda i, ft: (layer, i // nt, jnp.maximum(i % nt - ft, 0), 0),
                              ft=first_tile)))
    args = [x, p["g_mix"], p["w_main"], p["g_v"], p["g_q"], p["g_k"], p["cos_p"], p["sin_p"],
            p["mean_mat"]]
    n_in = len(args)
    aliases = {}
    if kv_prev is not None:
        for j, buf in enumerate(kv_prev):
            in_specs.append(pl.BlockSpec(memory_space=pl.ANY))
            aliases[len(args)] = n_plain_out + j
            args.append(buf)

    def body(*refs):
        _proj_kernel(*refs[:n_in], *refs[n_in + len(aliases):])

    return pl.pallas_call(
        body,
        grid=(n_tok // rows,),
        in_specs=in_specs,
        out_specs=out_specs,
        out_shape=out_shape,
        scratch_shapes=[pltpu.VMEM((n_strided, PAIR_WIDTH // V7X_LANES, rows, V7X_LANES), F32)],
        input_output_aliases=aliases,
        compiler_params=_params(),
        name=f"proj_prompt_l{layer}",
    )(*args)


def _attn_block(q, kk, vv, valid):
    nk = kk.shape[0]
    lane_head = lax.broadcasted_iota(jnp.int32, (1, PAIR_WIDTH), 1) // HEAD_DIM
    head_masks = [lane_head == h for h in range(HEADS_PER_PAIR)]
    qs = jnp.concatenate([jnp.where(hm, q, jnp.zeros_like(q)) for hm in head_masks], axis=0)
    s = lax.dot_general(qs, kk, (((1,), (1,)), ((), ())), preferred_element_type=F32)
    s = jnp.where(valid[None], s.reshape(HEADS_PER_PAIR, QBLK, nk), NEG_INF)
    mx = jnp.max(s, axis=-1, keepdims=True)
    p = jnp.exp2(s - mx)
    den = jnp.sum(p, axis=-1, keepdims=True)
    pv = jnp.dot(p.reshape(HEADS_PER_PAIR * QBLK, nk).astype(BF16), vv,
                 preferred_element_type=F32).reshape(HEADS_PER_PAIR, QBLK, PAIR_WIDTH)
    inv_den = 1.0 / den
    lse_h = mx * LN2 + jnp.log(den)
    o, inv, lse = pv[0], inv_den[0], lse_h[0]
    for h in range(1, HEADS_PER_PAIR):
        o = jnp.where(head_masks[h], pv[h], o)
        inv = jnp.where(head_masks[h], inv_den[h], inv)
        lse = jnp.where(head_masks[h], lse_h[h], lse)
    return o * inv, jnp.broadcast_to(lse, (QBLK, PAIR_WIDTH))


def _attn_kernel(q0_ref, k0_ref, v0_ref, q1_ref, k1_ref, v1_ref, q2_ref, k2_ref, v2_ref,
                 out_ref, o_scr, l_scr):
    qi = lax.broadcasted_iota(jnp.int32, (QBLK, QBLK), 0)
    ki = lax.broadcasted_iota(jnp.int32, (QBLK, QBLK), 1)
    causal = ki <= qi
    qi2 = lax.broadcasted_iota(jnp.int32, (QBLK, 2 * QBLK), 0)
    ki2 = lax.broadcasted_iota(jnp.int32, (QBLK, 2 * QBLK), 1)
    dist = qi2 - ki2 + QBLK
    band = (dist >= 0) & (dist <= TAPS)
    n_slabs = PAIR_WIDTH // V7X_LANES

    def put(scr, g, rows, val):
        for j in range(n_slabs):
            scr[g, j, rows, :] = val[:, j * V7X_LANES:(j + 1) * V7X_LANES]

    def get(scr, g, rows):
        return jnp.concatenate([scr[g, j, rows, :] for j in range(n_slabs)], axis=1)

    for g, refs in enumerate(((q0_ref, k0_ref, v0_ref), (q1_ref, k1_ref, v1_ref),
                              (q2_ref, k2_ref, v2_ref))):
        dil = DILATED_PAIRS[g][1]
        n_blocks = (SEQ // dil) // QBLK

        def rows_of(ref, r, start, dil=dil):
            if dil == 1:
                return ref[pl.ds(start, QBLK), :]
            return ref[0, r, pl.ds(start, QBLK), :]

        def step(idx, carry, g=g, dil=dil, n_blocks=n_blocks, refs=refs, rows_of=rows_of):
            q_ref, k_ref, v_ref = refs
            r, i = idx // n_blocks, idx % n_blocks
            cur = pl.multiple_of(i * QBLK, QBLK)
            q = rows_of(q_ref, r, cur)
            if n_blocks == 1:
                o, lse = _attn_block(q, rows_of(k_ref, r, cur), rows_of(v_ref, r, cur), causal)
            else:
                prev = pl.multiple_of(jnp.maximum(i - 1, 0) * QBLK, QBLK)
                kk = jnp.concatenate([rows_of(k_ref, r, prev), rows_of(k_ref, r, cur)], axis=0)
                vv = jnp.concatenate([rows_of(v_ref, r, prev), rows_of(v_ref, r, cur)], axis=0)
                first_key = jnp.where(i > 0, 0, QBLK)
                o, lse = _attn_block(q, kk, vv, band & (ki2 >= first_key))
            if dil == 1:
                rows = pl.ds(cur, QBLK)
            else:
                rows = pl.ds(r + i * (QBLK * dil), QBLK, stride=dil)
            put(o_scr, g, rows, o)
            put(l_scr, g, rows, lse)
            return carry

        lax.fori_loop(0, dil * n_blocks, step, 0, unroll=ATTN_UNROLL)

    step_rows = 256
    for t0 in range(0, SEQ, step_rows):
        rows = slice(t0, t0 + step_rows)
        l0, l1, l2 = get(l_scr, 0, rows), get(l_scr, 1, rows), get(l_scr, 2, rows)
        m = jnp.maximum(jnp.maximum(l0, l1), l2)
        e0, e1, e2 = jnp.exp(l0 - m), jnp.exp(l1 - m), jnp.exp(l2 - m)
        den = e0 + e1 + e2
        merged = ((e0 / den) * get(o_scr, 0, rows) + (e1 / den) * get(o_scr, 1, rows)
                  + (e2 / den) * get(o_scr, 2, rows))
        out_ref[rows, :] = merged.astype(out_ref.dtype)


def _attn_prompt(layer, qkv):
    n_tok = qkv[0].shape[0]
    in_specs = []
    for (window, dil) in DILATED_PAIRS:
        if dil == 1:
            in_specs += [pl.BlockSpec((SEQ, PAIR_WIDTH), lambda b: (b, 0))] * 3
        else:
            in_specs += [pl.BlockSpec((1, dil, SEQ // dil, PAIR_WIDTH),
                                      lambda b: (b, 0, 0, 0))] * 3
    scr = pltpu.VMEM((N_PAIRS, PAIR_WIDTH // V7X_LANES, SEQ, V7X_LANES), F32)
    return pl.pallas_call(
        _attn_kernel,
        grid=(BATCH,),
        in_specs=in_specs,
        out_specs=pl.BlockSpec((SEQ, PAIR_WIDTH), lambda b: (b, 0)),
        out_shape=jax.ShapeDtypeStruct((n_tok, PAIR_WIDTH), BF16),
        scratch_shapes=[scr, scr],
        compiler_params=_params(),
        name=f"attn_prompt_l{layer}",
    )(*qkv)


def _gated_merge(x, a_out, b_out, ga, gb, wa_ref, wb_ref, wo_ref):
    ap = jnp.dot(a_out.astype(BF16), wa_ref[...], preferred_element_type=F32)
    bp = jnp.dot(b_out.astype(BF16), wb_ref[...], preferred_element_type=F32)
    merged = ga * ap + gb * bp
    return x + jnp.dot(merged.astype(BF16), wo_ref[...], preferred_element_type=F32)


def _gates(xn, wg_ref):
    ga = jax.nn.sigmoid(jnp.dot(xn, wg_ref[:, :D_MODEL], preferred_element_type=F32))
    gb = jax.nn.sigmoid(jnp.dot(xn, wg_ref[:, D_MODEL:], preferred_element_type=F32))
    return ga, gb


def _mix_kernel(x_ref, u_ref, va_ref, bo_ref, gmix_ref, wg_ref, ws_ref, bias_ref,
                wa_ref, wb_ref, wo_ref, out_ref, a_scr):
    ti = lax.broadcasted_iota(jnp.int32, (CHUNK, CHUNK), 0)
    si = lax.broadcasted_iota(jnp.int32, (CHUNK, CHUNK), 1)
    tril = si <= ti
    w_tril = [jnp.where(tril, ws_ref[g], 0.0).astype(BF16) for g in range(A_GROUPS)]
    low_group = lax.broadcasted_iota(jnp.int32, (1, V7X_LANES), 1) < A_GROUP_DIM
    bias = bias_ref[...]
    groups_per_slab = V7X_LANES // A_GROUP_DIM
    for c in range(TOK_TILE // CHUNK):
        rows = slice(c * CHUNK, (c + 1) * CHUNK)
        zs = []
        for j in range(A_WIDTH // V7X_LANES):
            slab = va_ref[rows, j * V7X_LANES:(j + 1) * V7X_LANES]
            z_lo = jnp.dot(w_tril[groups_per_slab * j], slab, preferred_element_type=F32)
            z_hi = jnp.dot(w_tril[groups_per_slab * j + 1], slab, preferred_element_type=F32)
            zs.append(jnp.where(low_group, z_lo, z_hi))
        z = jnp.concatenate(zs, axis=1) + bias
        a_scr[rows, :] = (u_ref[rows, :].astype(F32) * z).astype(BF16)
    x = x_ref[...]
    ga, gb = _gates(_rms(x, gmix_ref[...]).astype(BF16), wg_ref)
    out_ref[...] = _gated_merge(x, a_scr[...], bo_ref[...], ga, gb, wa_ref, wb_ref, wo_ref)


def _mix_prompt(layer, x, u, va, b_out, p):
    rows = TOK_TILE
    n_tok = x.shape[0]

    def tok(width):
        return pl.BlockSpec((rows, width), lambda i: (i, 0))

    in_specs = [
        tok(D_MODEL), tok(A_WIDTH), tok(A_WIDTH), tok(PAIR_WIDTH),
        _layer_block((1, D_MODEL), layer),
        _layer_block((D_MODEL, 2 * D_MODEL), layer),
        _layer_block((A_GROUPS, CHUNK, CHUNK), layer),
        _layer_block((CHUNK, A_WIDTH), layer),
        _layer_block((A_WIDTH, D_MODEL), layer),
        _layer_block((PAIR_WIDTH, D_MODEL), layer),
        _layer_block((D_MODEL, D_MODEL), layer),
    ]
    return pl.pallas_call(
        _mix_kernel,
        grid=(n_tok // rows,),
        in_specs=in_specs,
        out_specs=tok(D_MODEL),
        out_shape=jax.ShapeDtypeStruct((n_tok, D_MODEL), F32),
        scratch_shapes=[pltpu.VMEM((rows, A_WIDTH), BF16)],
        compiler_params=_params(),
        name=f"mix_prompt_l{layer}",
    )(x, u, va, b_out, p["g_mix"], p["w_gates"], p["w_s"], p["bias_full"],
      p["w_a"], p["w_b"], p["w_o"])


def _ffn_kernel(x_ref, halo_ref, g_ref, wup_ref, cw_ref, cb_ref, wdown_ref,
                out_ref, conv_ref, xn_scr, up_scr, h_scr):
    rows = TOK_TILE
    tile = pl.program_id(0) % TILES_PER_SEQ
    x = x_ref[...]
    g = g_ref[...]
    halo_on = jnp.where(tile == 0, 0.0, 1.0)
    xn_scr[:HALO, :] = (_rms(halo_ref[...], g) * halo_on).astype(BF16)
    xn_scr[HALO:, :] = _rms(x, g).astype(BF16)
    xn = xn_scr[...]
    n_slabs = FF_CHUNK // V7X_LANES

    def up_and_conv(c0, slot):
        cols = slice(c0, c0 + FF_CHUNK)
        up = jnp.dot(xn, wup_ref[:, cols], preferred_element_type=F32)
        conv_ref[0, :, cols] = up[HALO + rows - (CONV_W - 1):]
        for s in range(n_slabs):
            up_scr[slot, s] = up[:, s * V7X_LANES:(s + 1) * V7X_LANES]
        parts = []
        for s in range(n_slabs):
            cs = slice(c0 + s * V7X_LANES, c0 + (s + 1) * V7X_LANES)
            acc = cb_ref[:, cs]
            for j in range(CONV_W - 1):
                lo = HALO - (CONV_W - 1) + j
                acc = acc + cw_ref[j:j + 1, cs] * up_scr[slot, s, lo:lo + rows, :]
            parts.append(acc + cw_ref[CONV_W - 1:CONV_W, cs]
                         * up[HALO:, s * V7X_LANES:(s + 1) * V7X_LANES])
        return jnp.concatenate(parts, axis=1)

    for j in range(N_FF_CHUNKS):
        c_gate = up_and_conv(j * FF_CHUNK, 0)
        c_val = up_and_conv(D_FF + j * FF_CHUNK, 1)
        h_scr[:, j * FF_CHUNK:(j + 1) * FF_CHUNK] = (jax.nn.silu(c_gate) * c_val).astype(BF16)
    out_ref[...] = x + jnp.dot(h_scr[...], wdown_ref[...], preferred_element_type=F32)


def _ffn_prompt(layer, x, p):
    rows = TOK_TILE
    n_tok = x.shape[0]
    nt = TILES_PER_SEQ
    halo_per_tile = rows // HALO
    in_specs = [
        pl.BlockSpec((rows, D_MODEL), lambda i: (i, 0)),
        pl.BlockSpec((HALO, D_MODEL), lambda i: (jnp.maximum(i * halo_per_tile - 1, 0), 0)),
        _layer_block((1, D_MODEL), layer),
        _layer_block((D_MODEL, 2 * D_FF), layer),
        _layer_block((CONV_W, 2 * D_FF), layer),
        _layer_block((1, 2 * D_FF), layer),
        _layer_block((D_FF, D_MODEL), layer),
    ]
    return pl.pallas_call(
        _ffn_kernel,
        grid=(n_tok // rows,),
        in_specs=in_specs,
        out_specs=[pl.BlockSpec((rows, D_MODEL), lambda i: (i, 0)),
                   pl.BlockSpec((1, CONV_W - 1, 2 * D_FF), lambda i: (i // nt, 0, 0))],
        out_shape=[jax.ShapeDtypeStruct((n_tok, D_MODEL), F32),
                   jax.ShapeDtypeStruct((BATCH, CONV_W - 1, 2 * D_FF), F32)],
        scratch_shapes=[pltpu.VMEM((HALO + rows, D_MODEL), BF16),
                        pltpu.VMEM((2, FF_CHUNK // V7X_LANES, HALO + rows, V7X_LANES), F32),
                        pltpu.VMEM((rows, D_FF), BF16)],
        compiler_params=_params(),
        name=f"ffn_prompt_l{layer}",
    )(x, x, p["g_ffn"], p["w_up"], p["conv_w"], p["conv_b"], p["w_down"])


def _sample_proj_kernel(x_ref, gmix_ref, w_ref, wg_ref, gv_ref, gq_ref, gk_ref, cos_ref, sin_ref,
                        mean_ref, u_ref, va_ref, q_ref, kv_ref, ga_ref, gb_ref):
    xn = _rms(x_ref[...], gmix_ref[...]).astype(BF16)

    def proj(c0, n):
        return jnp.dot(xn, w_ref[:, c0:c0 + n], preferred_element_type=F32)

    u_ref[...] = jax.nn.gelu(proj(OFF_U, A_WIDTH))
    va_ref[...] = _rms(jax.nn.gelu(proj(OFF_V, A_WIDTH)), gv_ref[...])
    ga_ref[...], gb_ref[...] = _gates(xn, wg_ref)
    for g in range(N_PAIRS):
        q, k, v = _qkv_pair(proj, g, gq_ref[...], gk_ref[...], mean_ref[...], cos_ref[...],
                            sin_ref[...])
        q_ref[:, g * PAIR_WIDTH:(g + 1) * PAIR_WIDTH] = q
        kv_ref[:, 2 * g * PAIR_WIDTH:(2 * g + 1) * PAIR_WIDTH] = k
        kv_ref[:, (2 * g + 1) * PAIR_WIDTH:(2 * g + 2) * PAIR_WIDTH] = v


def _proj_sample(layer, x, p):
    rows = x.shape[0]
    in_specs = [
        _full((rows, D_MODEL)),
        _layer_block((1, D_MODEL), layer),
        _layer_block((D_MODEL, MAIN_COLS), layer),
        _layer_block((D_MODEL, 2 * D_MODEL), layer),
        _layer_block((1, A_WIDTH), layer),
        _layer_block((1, PAIR_WIDTH), layer),
        _layer_block((1, PAIR_WIDTH), layer),
        _full((rows, PAIR_WIDTH)),
        _full((rows, PAIR_WIDTH)),
        _full((PAIR_WIDTH, PAIR_WIDTH)),
    ]
    widths = [A_WIDTH, A_WIDTH, B_WIDTH, 2 * B_WIDTH, D_MODEL, D_MODEL]
    return pl.pallas_call(
        _sample_proj_kernel,
        grid=(1,),
        in_specs=in_specs,
        out_specs=[_full((rows, w)) for w in widths],
        out_shape=[jax.ShapeDtypeStruct((rows, w), F32) for w in widths],
        compiler_params=_params(),
        name=f"proj_sample_l{layer}",
    )(x, p["g_mix"], p["w_main"], p["w_gates"], p["g_v"], p["g_q"], p["g_k"],
      p["cos_s"], p["sin_s"], p["mean_mat"])


def _sample_attn_kernel(qt_ref, kvt_ref, c0_ref, c1_ref, c2_ref, out_ref, o_scr, l_scr):
    for g, c_ref in enumerate((c0_ref, c1_ref, c2_ref)):
        window, dil = DILATED_PAIRS[g]
        pos = lax.broadcasted_iota(jnp.int32, (1, window), 1)
        tap = (pos % dil) == 0
        for b in range(SAMPLE_GROUP):
            col = slice(b, b + 1)
            for h in range(HEADS_PER_PAIR):
                feat = slice(g * PAIR_WIDTH + h * HEAD_DIM, g * PAIR_WIDTH + (h + 1) * HEAD_DIM)
                kfeat = slice(2 * g * PAIR_WIDTH + h * HEAD_DIM,
                              2 * g * PAIR_WIDTH + (h + 1) * HEAD_DIM)
                vfeat = slice((2 * g + 1) * PAIR_WIDTH + h * HEAD_DIM,
                              (2 * g + 1) * PAIR_WIDTH + (h + 1) * HEAD_DIM)
                q = qt_ref[0, feat, col]
                k_new = kvt_ref[0, kfeat, col]
                v_new = kvt_ref[0, vfeat, col]
                s = jnp.sum(c_ref[0, b, 0, h] * q, axis=0, keepdims=True)
                s = jnp.where(tap, s, NEG_INF)
                s_new = jnp.sum(k_new * q, axis=0, keepdims=True)
                mx = jnp.maximum(jnp.max(s, axis=1, keepdims=True), s_new)
                pr = jnp.exp(s - mx)
                p_new = jnp.exp(s_new - mx)
                den = jnp.sum(pr, axis=1, keepdims=True) + p_new
                o = (jnp.sum(c_ref[0, b, 1, h] * pr, axis=1, keepdims=True) + p_new * v_new) / den
                out_rows = slice(h * HEAD_DIM, (h + 1) * HEAD_DIM)
                o_scr[g, out_rows, col] = o
                l_scr[g, out_rows, col] = jnp.broadcast_to(mx + jnp.log(den), (HEAD_DIM, 1))
    l0, l1, l2 = l_scr[0], l_scr[1], l_scr[2]
    m = jnp.maximum(jnp.maximum(l0, l1), l2)
    e0, e1, e2 = jnp.exp(l0 - m), jnp.exp(l1 - m), jnp.exp(l2 - m)
    den = e0 + e1 + e2
    out_ref[0] = (e0 / den) * o_scr[0] + (e1 / den) * o_scr[1] + (e2 / den) * o_scr[2]


def _attn_sample(layer, qt, kvt, caches):
    n = SAMPLE_GROUP
    groups = qt.shape[0]
    in_specs = [pl.BlockSpec((1, B_WIDTH, n), lambda i: (i, 0, 0)),
                pl.BlockSpec((1, 2 * B_WIDTH, n), lambda i: (i, 0, 0))]
    for (window, dil) in DILATED_PAIRS:
        in_specs.append(pl.BlockSpec((1, n, 2, HEADS_PER_PAIR, HEAD_DIM, window),
                                     lambda i: (layer, i, 0, 0, 0, 0)))
    scr = pltpu.VMEM((N_PAIRS, PAIR_WIDTH, n), F32)
    return pl.pallas_call(
        _sample_attn_kernel,
        grid=(groups,),
        in_specs=in_specs,
        out_specs=pl.BlockSpec((1, PAIR_WIDTH, n), lambda i: (i, 0, 0)),
        out_shape=jax.ShapeDtypeStruct((groups, PAIR_WIDTH, n), F32),
        scratch_shapes=[scr, scr],
        compiler_params=_params(),
        name=f"attn_sample_l{layer}",
    )(qt, kvt, *caches)


def _sample_mix_kernel(x_ref, u_ref, va_ref, bo_ref, ga_ref, gb_ref, wdiag_ref, bias0_ref,
                       wa_ref, wb_ref, wo_ref, out_ref):
    a_out = u_ref[...] * (wdiag_ref[...] * va_ref[...] + bias0_ref[...])
    out_ref[...] = _gated_merge(x_ref[...], a_out, bo_ref[...], ga_ref[...], gb_ref[...],
                                wa_ref, wb_ref, wo_ref)


def _mix_sample(layer, x, u, va, b_out, ga, gb, p):
    rows = x.shape[0]
    in_specs = [_full((rows, D_MODEL)), _full((rows, A_WIDTH)), _full((rows, A_WIDTH)),
                _full((rows, PAIR_WIDTH)), _full((rows, D_MODEL)), _full((rows, D_MODEL)),
                _layer_block((1, A_WIDTH), layer), _layer_block((1, A_WIDTH), layer),
                _layer_block((A_WIDTH, D_MODEL), layer),
                _layer_block((PAIR_WIDTH, D_MODEL), layer),
                _layer_block((D_MODEL, D_MODEL), layer)]
    return pl.pallas_call(
        _sample_mix_kernel,
        grid=(1,),
        in_specs=in_specs,
        out_specs=_full((rows, D_MODEL)),
        out_shape=jax.ShapeDtypeStruct((rows, D_MODEL), F32),
        compiler_params=_params(),
        name=f"mix_sample_l{layer}",
    )(x, u, va, b_out, ga, gb, p["wdiag"], p["bias0"], p["w_a"], p["w_b"], p["w_o"])


def _sample_ffn_kernel(x_ref, g_ref, wg_ref, wv_ref, cwg_ref, cwv_ref, cbg_ref, cbv_ref,
                       h0g_ref, h0v_ref, h1g_ref, h1v_ref, wdown_ref,
                       out_ref, upg_ref, upv_ref):
    j = pl.program_id(0)
    x = x_ref[...]
    xn = _rms(x, g_ref[...]).astype(BF16)

    def up_and_conv(w_ref, cw_ref, cb_ref, h0_ref, h1_ref, up_ref):
        up = jnp.dot(xn, w_ref[...], preferred_element_type=F32)
        up_ref[...] = up
        return (cb_ref[...] + cw_ref[0:1, :] * h0_ref[...] + cw_ref[1:2, :] * h1_ref[...]
                + cw_ref[2:3, :] * up)

    c_gate = up_and_conv(wg_ref, cwg_ref, cbg_ref, h0g_ref, h1g_ref, upg_ref)
    c_val = up_and_conv(wv_ref, cwv_ref, cbv_ref, h0v_ref, h1v_ref, upv_ref)
    h = (jax.nn.silu(c_gate) * c_val).astype(BF16)
    part = jnp.dot(h, wdown_ref[...], preferred_element_type=F32)

    @pl.when(j == 0)
    def _():
        out_ref[...] = x + part

    @pl.when(j > 0)
    def _():
        out_ref[...] += part


def _ffn_sample(layer, x, p, state_conv):
    rows = x.shape[0]
    nch = N_FF_CHUNKS
    fc = FF_CHUNK

    def cols(nrows, blk):
        return pl.BlockSpec((None, nrows, fc), lambda j: (layer, 0, blk * nch + j))

    state_flat = state_conv.reshape(DEPTH, rows, (CONV_W - 1) * 2 * D_FF)
    hist = [cols(rows, 2 * row + half) for row in range(CONV_W - 1) for half in range(2)]
    in_specs = [
        _full((rows, D_MODEL)),
        _layer_block((1, D_MODEL), layer),
        cols(D_MODEL, 0), cols(D_MODEL, 1),
        cols(CONV_W, 0), cols(CONV_W, 1),
        cols(1, 0), cols(1, 1),
        *hist,
        pl.BlockSpec((None, fc, D_MODEL), lambda j: (layer, j, 0)),
    ]
    up_spec = pl.BlockSpec((rows, fc), lambda j: (0, j))
    up_shape = jax.ShapeDtypeStruct((rows, D_FF), F32)
    return pl.pallas_call(
        _sample_ffn_kernel,
        grid=(nch,),
        in_specs=in_specs,
        out_specs=[_full((rows, D_MODEL)), up_spec, up_spec],
        out_shape=[jax.ShapeDtypeStruct((rows, D_MODEL), F32), up_shape, up_shape],
        compiler_params=_params(),
        name=f"ffn_sample_l{layer}",
    )(x, p["g_ffn"], p["w_up"], p["w_up"], p["conv_w"], p["conv_w"], p["conv_b"], p["conv_b"],
      state_flat, state_flat, state_flat, state_flat, p["w_down"])


def _sample_layer(layer, xs, p, caches_t, state_conv):
    n = SAMPLE_GROUP
    rows = xs.shape[0]
    u, va, q, kv, ga, gb = _proj_sample(layer, xs, p)

    def columns(a):
        return a.reshape(rows // n, n, a.shape[1]).swapaxes(1, 2)

    bt = _attn_sample(layer, columns(q), columns(kv), caches_t)
    b_out = bt.swapaxes(1, 2).reshape(rows, PAIR_WIDTH)
    xs = _mix_sample(layer, xs, u, va, b_out, ga, gb, p)
    xs, upg, upv = _ffn_sample(layer, xs, p, state_conv)
    up_row = jnp.concatenate([upg, upv], axis=-1)[:, None, :]
    new_conv = jnp.concatenate([state_conv[layer][:, 1:], up_row], axis=1)
    return xs, kv, va, new_conv


def _rope_tables(pos):
    half = HEAD_DIM // 2
    inv_freq = ROPE_THETA ** (-jnp.arange(half, dtype=F32) / half)
    ang = pos.astype(F32)[:, None] * inv_freq[None, :]
    cos, sin = jnp.cos(ang), jnp.sin(ang)
    cos_full = jnp.tile(jnp.concatenate([cos, cos], axis=1), (1, HEADS_PER_PAIR))
    sin_signed = jnp.tile(jnp.concatenate([-sin, sin], axis=1), (1, HEADS_PER_PAIR))
    return cos_full, sin_signed


def _prepare_params(g_mix, w_in, g_v, w_s, b_s, g_q, g_k, w_a_proj, w_b_proj, w_o,
                    g_ffn, w_up, conv_w, conv_b, w_down):
    head_of = jnp.arange(PAIR_WIDTH) // HEAD_DIM
    same_head = head_of[:, None] == head_of[None, :]
    bias_full = jnp.repeat(jnp.swapaxes(b_s, 1, 2), A_GROUP_DIM, axis=2)
    cos_p, sin_p = _rope_tables(jnp.arange(SEQ))
    cos_s, sin_s = _rope_tables(jnp.full((DEC_BATCH,), PAST_LEN))
    return dict(
        g_mix=g_mix[:, None, :], g_v=g_v[:, None, :], g_ffn=g_ffn[:, None, :],
        g_q=jnp.tile(g_q, (1, HEADS_PER_PAIR))[:, None, :],
        g_k=jnp.tile(g_k, (1, HEADS_PER_PAIR))[:, None, :],
        w_main=w_in[:, :, :MAIN_COLS].astype(BF16),
        w_gates=w_in[:, :, OFF_GATES:].astype(BF16),
        w_a=w_a_proj.astype(BF16), w_b=w_b_proj.astype(BF16), w_o=w_o.astype(BF16),
        w_up=w_up.astype(BF16), w_down=w_down.astype(BF16),
        w_s=w_s, bias_full=bias_full, bias0=bias_full[:, 0:1, :],
        wdiag=jnp.repeat(w_s[:, :, 0, 0], A_GROUP_DIM, axis=1)[:, None, :],
        conv_w=conv_w, conv_b=conv_b[:, None, :],
        mean_mat=jnp.where(same_head, 1.0 / HEAD_DIM, 0.0).astype(BF16),
        cos_p=cos_p, sin_p=sin_p, cos_s=cos_s, sin_s=sin_s,
    )


def kernel(x_prompt, x_sample, cache_kv_w128, cache_kv_w512, cache_kv_w2048, state_conv,
           g_mix, w_in, g_v, w_s, b_s, g_q, g_k, w_a_proj, w_b_proj, w_o,
           g_ffn, w_up, conv_w, conv_b, w_down):
    assert x_prompt.shape == (BATCH, SEQ, D_MODEL) and x_sample.shape == (DEC_BATCH, 1, D_MODEL)
    p = _prepare_params(g_mix, w_in, g_v, w_s, b_s, g_q, g_k, w_a_proj, w_b_proj, w_o,
                        g_ffn, w_up, conv_w, conv_b, w_down)
    caches_t = []
    for cache, (window, dil) in zip((cache_kv_w128, cache_kv_w512, cache_kv_w2048), DILATED_PAIRS):
        assert cache.shape[2] == window
        caches_t.append(jnp.transpose(cache, (0, 1, 3, 4, 5, 2)))

    xp = x_prompt.reshape(BATCH * SEQ, D_MODEL)
    xs = x_sample.reshape(DEC_BATCH, D_MODEL)
    kv_prompt = None
    conv_p, conv_s, kv_s, va_s = [], [], [], []
    for l in range(DEPTH):
        outs = _proj_prompt(l, xp, p, kv_prompt)
        u, va = outs[0], outs[1]
        qkv = outs[2:11]
        kv_prompt = outs[11:14]
        b_out = _attn_prompt(l, qkv)
        xp = _mix_prompt(l, xp, u, va, b_out, p)
        xp, nconv = _ffn_prompt(l, xp, p)
        conv_p.append(nconv)

        xs, skv, sva, sconv = _sample_layer(l, xs, p, caches_t, state_conv)
        kv_s.append(skv)
        va_s.append(sva)
        conv_s.append(sconv)

    def kv_shape(a, rows):
        return a.reshape(DEPTH, -1, rows, 2, HEADS_PER_PAIR, HEAD_DIM)

    kv_s = jnp.stack(kv_s, axis=0)
    kv_s = [kv_s[:, :, 2 * g * PAIR_WIDTH:(2 * g + 2) * PAIR_WIDTH] for g in range(N_PAIRS)]
    return (
        xp.reshape(BATCH, SEQ, D_MODEL),
        xs.reshape(DEC_BATCH, 1, D_MODEL),
        kv_shape(kv_prompt[0], min(DILATED_PAIRS[0][0], SEQ)),
        kv_shape(kv_prompt[1], min(DILATED_PAIRS[1][0], SEQ)),
        kv_shape(kv_prompt[2], min(DILATED_PAIRS[2][0], SEQ)),
        jnp.stack(conv_p, axis=0),
        kv_shape(kv_s[0], 1),
        kv_shape(kv_s[1], 1),
        kv_shape(kv_s[2], 1),
        jnp.stack(conv_s, axis=0),
        jnp.stack(va_s, axis=0).reshape(DEPTH, DEC_BATCH, 1, A_WIDTH),
    )
```

```python
import functools

import jax
import jax.numpy as jnp
from jax import lax
from jax.experimental import pallas as pl
from jax.experimental.pallas import tpu as pltpu

D_MODEL = 1024
BATCH = 16
SEQ = 2048
DEPTH = 2
DEC_BATCH = 32
PAST_LEN = 16384
CHUNK = 128
A_GROUPS = 8
A_GROUP_DIM = 64
A_WIDTH = A_GROUPS * A_GROUP_DIM
HEAD_DIM = 64
HEADS_PER_PAIR = 4
DILATED_PAIRS = ((128, 1), (512, 4), (2048, 16))
N_PAIRS = len(DILATED_PAIRS)
TAPS = DILATED_PAIRS[0][0] // DILATED_PAIRS[0][1]
PAIR_WIDTH = HEADS_PER_PAIR * HEAD_DIM
B_WIDTH = N_PAIRS * PAIR_WIDTH
QBLK = 128
D_FF = 2816
CONV_W = 3
ROPE_THETA = 10000.0
EPS = 1e-6
NEG_INF = -1e30
LOG2E = 1.4426950408889634
LN2 = 0.6931471805599453
OFF_U, OFF_V = 0, A_WIDTH
OFF_Q = 2 * A_WIDTH
OFF_K = OFF_Q + B_WIDTH
OFF_VV = OFF_K + B_WIDTH
OFF_GATES = OFF_VV + B_WIDTH
MAIN_COLS = OFF_GATES

V7X_LANES = 128
V7X_VMEM_BYTES = 64 * 1024 * 1024
VMEM_LIMIT = V7X_VMEM_BYTES - 8 * 1024 * 1024

TOK_TILE = 512
TILES_PER_SEQ = SEQ // TOK_TILE
PROJ_TILE = 512
HALO = 16
FF_CHUNK = 256
N_FF_CHUNKS = D_FF // FF_CHUNK
ATTN_UNROLL = 8
SAMPLE_GROUP = 2

F32 = jnp.float32
BF16 = jnp.bfloat16

assert all(w // d == TAPS for (w, d) in DILATED_PAIRS) and TAPS >= QBLK - 1


def _params(**kw):
    return pltpu.CompilerParams(dimension_semantics=("arbitrary",),
                                vmem_limit_bytes=VMEM_LIMIT, **kw)


def _rms(x, g):
    return x * lax.rsqrt(jnp.mean(x * x, axis=-1, keepdims=True) + EPS) * g


def _swap_halves(t):
    lane = lax.broadcasted_iota(jnp.int32, (1, V7X_LANES), 1)
    first_half = (lane % HEAD_DIM) < (HEAD_DIM // 2)
    slabs = []
    for j in range(t.shape[1] // V7X_LANES):
        s = t[:, j * V7X_LANES:(j + 1) * V7X_LANES]
        up = pltpu.roll(s, V7X_LANES - HEAD_DIM // 2, 1)
        down = pltpu.roll(s, HEAD_DIM // 2, 1)
        slabs.append(jnp.where(first_half, up, down))
    return jnp.concatenate(slabs, axis=1)


def _head_norm_rope(t, gain, mean_mat, cos, sin_signed):
    ms = jnp.dot((t * t).astype(BF16), mean_mat, preferred_element_type=F32)
    tn = t * lax.rsqrt(ms + EPS) * gain
    return tn * cos + _swap_halves(tn) * sin_signed


def _full(shape):
    return pl.BlockSpec(shape, lambda i: (0,) * len(shape))


def _layer_block(shape, layer):
    return pl.BlockSpec((None,) + tuple(shape), lambda i: (layer,) + (0,) * len(shape))


def _qkv_pair(proj, g, gq, gk, mean_mat, cos, sin_signed, q_scale=HEAD_DIM ** -0.5):
    c = g * PAIR_WIDTH
    q = _head_norm_rope(proj(OFF_Q + c, PAIR_WIDTH), gq, mean_mat, cos, sin_signed)
    q = q * q_scale
    k = _head_norm_rope(proj(OFF_K + c, PAIR_WIDTH), gk, mean_mat, cos, sin_signed)
    v = proj(OFF_VV + c, PAIR_WIDTH)
    return q, k, v


def _proj_kernel(x_ref, gmix_ref, w_ref, gv_ref, gq_ref, gk_ref, cos_ref, sin_ref, mean_ref,
                 *refs, n_prev):
    prev_kv_refs = refs[:N_PAIRS] if n_prev else (None,) * N_PAIRS
    (u_ref, va_ref, q0_ref, k0_ref, v0_ref, q1_ref, k1_ref, v1_ref, q2_ref, k2_ref, v2_ref,
     kv0_ref, kv1_ref, kv2_ref, scr_ref) = refs[N_PAIRS if n_prev else 0:]
    rows = PROJ_TILE
    qkv_refs = ((q0_ref, k0_ref, v0_ref), (q1_ref, k1_ref, v1_ref), (q2_ref, k2_ref, v2_ref))
    kv_refs = (kv0_ref, kv1_ref, kv2_ref)
    xn = _rms(x_ref[...], gmix_ref[...]).astype(BF16)

    def proj(c0, n):
        return jnp.dot(xn, w_ref[:, c0:c0 + n], preferred_element_type=F32)

    cos = cos_ref[...]
    sin_signed = sin_ref[...]
    mean_mat = mean_ref[...]
    n_slabs = PAIR_WIDTH // V7X_LANES

    def store_residue_major(dst_ref, slot, val, dil):
        for j in range(n_slabs):
            scr_ref[slot, j] = val[:, j * V7X_LANES:(j + 1) * V7X_LANES]
        for r in range(dil):
            picked = [scr_ref[slot, j, pl.ds(r, rows // dil, stride=dil), :]
                      for j in range(n_slabs)]
            dst_ref[0, r] = jnp.concatenate(picked, axis=1).astype(dst_ref.dtype)

    slot = 0
    for g, (window, dil) in reversed(list(enumerate(DILATED_PAIRS))):
        q, k, v = _qkv_pair(proj, g, gq_ref[...], gk_ref[...], mean_mat, cos, sin_signed,
                            q_scale=HEAD_DIM ** -0.5 * LOG2E)
        keep = min(min(window, SEQ), rows)
        if n_prev:
            kv_refs[g][:n_prev] = prev_kv_refs[g][...]
        kv_refs[g][n_prev, 0, :, :PAIR_WIDTH] = k[rows - keep:]
        kv_refs[g][n_prev, 0, :, PAIR_WIDTH:] = v[rows - keep:]
        for ref, val in zip(qkv_refs[g], (q, k, v)):
            if dil == 1:
                ref[...] = val.astype(ref.dtype)
            else:
                store_residue_major(ref, slot, val, dil)
                slot += 1
    va_ref[...] = _rms(jax.nn.gelu(proj(OFF_V, A_WIDTH)), gv_ref[...]).astype(va_ref.dtype)
    u_ref[...] = jax.nn.gelu(proj(OFF_U, A_WIDTH)).astype(u_ref.dtype)


def _proj_prompt(layer, x, p, kv_prev):
    rows = PROJ_TILE
    n_tok = x.shape[0]
    nt = SEQ // rows

    def tok(width):
        return pl.BlockSpec((rows, width), lambda i: (i, 0))

    in_specs = [
        tok(D_MODEL),
        _layer_block((1, D_MODEL), layer),
        _layer_block((D_MODEL, MAIN_COLS), layer),
        _layer_block((1, A_WIDTH), layer),
        _layer_block((1, PAIR_WIDTH), layer),
        _layer_block((1, PAIR_WIDTH), layer),
        pl.BlockSpec((rows, PAIR_WIDTH), lambda i: (i % nt, 0)),
        pl.BlockSpec((rows, PAIR_WIDTH), lambda i: (i % nt, 0)),
        _full((PAIR_WIDTH, PAIR_WIDTH)),
    ]
    out_shape = [jax.ShapeDtypeStruct((n_tok, A_WIDTH), BF16)] * 2
    out_specs = [tok(A_WIDTH)] * 2
    n_strided = 0
    for (window, dil) in DILATED_PAIRS:
        if dil == 1:
            out_shape += [jax.ShapeDtypeStruct((n_tok, PAIR_WIDTH), BF16)] * 3
            out_specs += [tok(PAIR_WIDTH)] * 3
        else:
            n_strided += 3
            out_shape += [jax.ShapeDtypeStruct((BATCH, dil, SEQ // dil, PAIR_WIDTH), BF16)] * 3
            out_specs += [pl.BlockSpec((1, dil, rows // dil, PAIR_WIDTH),
                                       lambda i: (i // nt, 0, i % nt, 0))] * 3
    args = [x, p["g_mix"], p["w_main"], p["g_v"], p["g_q"], p["g_k"], p["cos_p"], p["sin_p"],
            p["mean_mat"]]
    for g, (window, dil) in enumerate(DILATED_PAIRS):
        keep = min(window, SEQ)
        first_tile = (SEQ - max(keep, rows)) // rows

        def kv_spec(n_layers, ft=first_tile, blk=min(keep, rows)):
            return pl.BlockSpec((n_layers, 1, blk, 2 * PAIR_WIDTH),
                                lambda i: (0, i // nt, jnp.maximum(i % nt - ft, 0), 0))

        out_shape.append(jax.ShapeDtypeStruct((layer + 1, BATCH, keep, 2 * PAIR_WIDTH), F32))
        out_specs.append(kv_spec(layer + 1))
        if layer > 0:
            in_specs.append(kv_spec(layer))
            args.append(kv_prev[g])

    return pl.pallas_call(
        functools.partial(_proj_kernel, n_prev=layer),
        grid=(n_tok // rows,),
        in_specs=in_specs,
        out_specs=out_specs,
        out_shape=out_shape,
        scratch_shapes=[pltpu.VMEM((n_strided, PAIR_WIDTH // V7X_LANES, rows, V7X_LANES), F32)],
        compiler_params=_params(),
        name=f"proj_prompt_l{layer}",
    )(*args)


def _attn_block(q, kk, vv, valid):
    nk = kk.shape[0]
    lane_head = lax.broadcasted_iota(jnp.int32, (1, PAIR_WIDTH), 1) // HEAD_DIM
    head_masks = [lane_head == h for h in range(HEADS_PER_PAIR)]
    qs = jnp.concatenate([jnp.where(hm, q, jnp.zeros_like(q)) for hm in head_masks], axis=0)
    s = lax.dot_general(qs, kk, (((1,), (1,)), ((), ())), preferred_element_type=F32)
    s = jnp.where(valid[None], s.reshape(HEADS_PER_PAIR, QBLK, nk), NEG_INF)
    mx = jnp.max(s, axis=-1, keepdims=True)
    p = jnp.exp2(s - mx)
    den = jnp.sum(p, axis=-1, keepdims=True)
    pv = jnp.dot(p.reshape(HEADS_PER_PAIR * QBLK, nk).astype(BF16), vv,
                 preferred_element_type=F32).reshape(HEADS_PER_PAIR, QBLK, PAIR_WIDTH)
    inv_den = 1.0 / den
    lse_h = mx * LN2 + jnp.log(den)
    o, inv, lse = pv[0], inv_den[0], lse_h[0]
    for h in range(1, HEADS_PER_PAIR):
        o = jnp.where(head_masks[h], pv[h], o)
        inv = jnp.where(head_masks[h], inv_den[h], inv)
        lse = jnp.where(head_masks[h], lse_h[h], lse)
    return o * inv, jnp.broadcast_to(lse, (QBLK, PAIR_WIDTH))


def _attn_kernel(q0_ref, k0_ref, v0_ref, q1_ref, k1_ref, v1_ref, q2_ref, k2_ref, v2_ref,
                 out_ref, o_scr, l_scr):
    qi = lax.broadcasted_iota(jnp.int32, (QBLK, QBLK), 0)
    ki = lax.broadcasted_iota(jnp.int32, (QBLK, QBLK), 1)
    causal = ki <= qi
    qi2 = lax.broadcasted_iota(jnp.int32, (QBLK, 2 * QBLK), 0)
    ki2 = lax.broadcasted_iota(jnp.int32, (QBLK, 2 * QBLK), 1)
    dist = qi2 - ki2 + QBLK
    band = (dist >= 0) & (dist <= TAPS)
    n_slabs = PAIR_WIDTH // V7X_LANES

    def put(scr, g, rows, val):
        for j in range(n_slabs):
            scr[g, j, rows, :] = val[:, j * V7X_LANES:(j + 1) * V7X_LANES]

    def get(scr, g, rows):
        return jnp.concatenate([scr[g, j, rows, :] for j in range(n_slabs)], axis=1)

    for g, refs in enumerate(((q0_ref, k0_ref, v0_ref), (q1_ref, k1_ref, v1_ref),
                              (q2_ref, k2_ref, v2_ref))):
        dil = DILATED_PAIRS[g][1]
        n_blocks = (SEQ // dil) // QBLK

        def rows_of(ref, r, start, dil=dil):
            if dil == 1:
                return ref[pl.ds(start, QBLK), :]
            return ref[0, r, pl.ds(start, QBLK), :]

        def step(idx, carry, g=g, dil=dil, n_blocks=n_blocks, refs=refs, rows_of=rows_of):
            q_ref, k_ref, v_ref = refs
            r, i = idx // n_blocks, idx % n_blocks
            cur = pl.multiple_of(i * QBLK, QBLK)
            q = rows_of(q_ref, r, cur)
            if n_blocks == 1:
                o, lse = _attn_block(q, rows_of(k_ref, r, cur), rows_of(v_ref, r, cur), causal)
            else:
                prev = pl.multiple_of(jnp.maximum(i - 1, 0) * QBLK, QBLK)
                kk = jnp.concatenate([rows_of(k_ref, r, prev), rows_of(k_ref, r, cur)], axis=0)
                vv = jnp.concatenate([rows_of(v_ref, r, prev), rows_of(v_ref, r, cur)], axis=0)
                first_key = jnp.where(i > 0, 0, QBLK)
                o, lse = _attn_block(q, kk, vv, band & (ki2 >= first_key))
            if dil == 1:
                rows = pl.ds(cur, QBLK)
            else:
                rows = pl.ds(r + i * (QBLK * dil), QBLK, stride=dil)
            put(o_scr, g, rows, o)
            put(l_scr, g, rows, lse)
            return carry

        lax.fori_loop(0, dil * n_blocks, step, 0, unroll=ATTN_UNROLL)

    step_rows = 256
    for t0 in range(0, SEQ, step_rows):
        rows = slice(t0, t0 + step_rows)
        l0, l1, l2 = get(l_scr, 0, rows), get(l_scr, 1, rows), get(l_scr, 2, rows)
        m = jnp.maximum(jnp.maximum(l0, l1), l2)
        e0, e1, e2 = jnp.exp(l0 - m), jnp.exp(l1 - m), jnp.exp(l2 - m)
        den = e0 + e1 + e2
        merged = ((e0 / den) * get(o_scr, 0, rows) + (e1 / den) * get(o_scr, 1, rows)
                  + (e2 / den) * get(o_scr, 2, rows))
        out_ref[rows, :] = merged.astype(out_ref.dtype)


def _attn_prompt(layer, qkv):
    n_tok = qkv[0].shape[0]
    in_specs = []
    for (window, dil) in DILATED_PAIRS:
        if dil == 1:
            in_specs += [pl.BlockSpec((SEQ, PAIR_WIDTH), lambda b: (b, 0))] * 3
        else:
            in_specs += [pl.BlockSpec((1, dil, SEQ // dil, PAIR_WIDTH),
                                      lambda b: (b, 0, 0, 0))] * 3
    scr = pltpu.VMEM((N_PAIRS, PAIR_WIDTH // V7X_LANES, SEQ, V7X_LANES), F32)
    return pl.pallas_call(
        _attn_kernel,
        grid=(BATCH,),
        in_specs=in_specs,
        out_specs=pl.BlockSpec((SEQ, PAIR_WIDTH), lambda b: (b, 0)),
        out_shape=jax.ShapeDtypeStruct((n_tok, PAIR_WIDTH), BF16),
        scratch_shapes=[scr, scr],
        compiler_params=_params(),
        name=f"attn_prompt_l{layer}",
    )(*qkv)


def _gated_merge(x, a_out, b_out, ga, gb, wa_ref, wb_ref, wo_ref):
    ap = jnp.dot(a_out.astype(BF16), wa_ref[...], preferred_element_type=F32)
    bp = jnp.dot(b_out.astype(BF16), wb_ref[...], preferred_element_type=F32)
    merged = ga * ap + gb * bp
    return x + jnp.dot(merged.astype(BF16), wo_ref[...], preferred_element_type=F32)


def _gates(xn, wg_ref):
    ga = jax.nn.sigmoid(jnp.dot(xn, wg_ref[:, :D_MODEL], preferred_element_type=F32))
    gb = jax.nn.sigmoid(jnp.dot(xn, wg_ref[:, D_MODEL:], preferred_element_type=F32))
    return ga, gb


def _mix_kernel(x_ref, u_ref, va_ref, bo_ref, gmix_ref, wg_ref, ws_ref, bias_ref,
                wa_ref, wb_ref, wo_ref, out_ref, a_scr):
    ti = lax.broadcasted_iota(jnp.int32, (CHUNK, CHUNK), 0)
    si = lax.broadcasted_iota(jnp.int32, (CHUNK, CHUNK), 1)
    tril = si <= ti
    w_tril = [jnp.where(tril, ws_ref[g], 0.0).astype(BF16) for g in range(A_GROUPS)]
    low_group = lax.broadcasted_iota(jnp.int32, (1, V7X_LANES), 1) < A_GROUP_DIM
    bias = bias_ref[...]
    groups_per_slab = V7X_LANES // A_GROUP_DIM
    for c in range(TOK_TILE // CHUNK):
        rows = slice(c * CHUNK, (c + 1) * CHUNK)
        zs = []
        for j in range(A_WIDTH // V7X_LANES):
            slab = va_ref[rows, j * V7X_LANES:(j + 1) * V7X_LANES]
            z_lo = jnp.dot(w_tril[groups_per_slab * j], slab, preferred_element_type=F32)
            z_hi = jnp.dot(w_tril[groups_per_slab * j + 1], slab, preferred_element_type=F32)
            zs.append(jnp.where(low_group, z_lo, z_hi))
        z = jnp.concatenate(zs, axis=1) + bias
        a_scr[rows, :] = (u_ref[rows, :].astype(F32) * z).astype(BF16)
    x = x_ref[...]
    ga, gb = _gates(_rms(x, gmix_ref[...]).astype(BF16), wg_ref)
    out_ref[...] = _gated_merge(x, a_scr[...], bo_ref[...], ga, gb, wa_ref, wb_ref, wo_ref)


def _mix_prompt(layer, x, u, va, b_out, p):
    rows = TOK_TILE
    n_tok = x.shape[0]

    def tok(width):
        return pl.BlockSpec((rows, width), lambda i: (i, 0))

    in_specs = [
        tok(D_MODEL), tok(A_WIDTH), tok(A_WIDTH), tok(PAIR_WIDTH),
        _layer_block((1, D_MODEL), layer),
        _layer_block((D_MODEL, 2 * D_MODEL), layer),
        _layer_block((A_GROUPS, CHUNK, CHUNK), layer),
        _layer_block((CHUNK, A_WIDTH), layer),
        _layer_block((A_WIDTH, D_MODEL), layer),
        _layer_block((PAIR_WIDTH, D_MODEL), layer),
        _layer_block((D_MODEL, D_MODEL), layer),
    ]
    return pl.pallas_call(
        _mix_kernel,
        grid=(n_tok // rows,),
        in_specs=in_specs,
        out_specs=tok(D_MODEL),
        out_shape=jax.ShapeDtypeStruct((n_tok, D_MODEL), F32),
        scratch_shapes=[pltpu.VMEM((rows, A_WIDTH), BF16)],
        compiler_params=_params(),
        name=f"mix_prompt_l{layer}",
    )(x, u, va, b_out, p["g_mix"], p["w_gates"], p["w_s"], p["bias_full"],
      p["w_a"], p["w_b"], p["w_o"])


def _ffn_kernel(x_ref, halo_ref, g_ref, wup_ref, cw_ref, cb_ref, wdown_ref,
                out_ref, conv_ref, xn_scr, up_scr, h_scr):
    rows = TOK_TILE
    tile = pl.program_id(0) % TILES_PER_SEQ
    x = x_ref[...]
    g = g_ref[...]
    halo_on = jnp.where(tile == 0, 0.0, 1.0)
    xn_scr[:HALO, :] = (_rms(halo_ref[...], g) * halo_on).astype(BF16)
    xn_scr[HALO:, :] = _rms(x, g).astype(BF16)
    xn = xn_scr[...]
    n_slabs = FF_CHUNK // V7X_LANES

    def up_and_conv(c0, slot):
        cols = slice(c0, c0 + FF_CHUNK)
        up = jnp.dot(xn, wup_ref[:, cols], preferred_element_type=F32)
        conv_ref[0, :, cols] = up[HALO + rows - (CONV_W - 1):]
        for s in range(n_slabs):
            up_scr[slot, s] = up[:, s * V7X_LANES:(s + 1) * V7X_LANES]
        parts = []
        for s in range(n_slabs):
            cs = slice(c0 + s * V7X_LANES, c0 + (s + 1) * V7X_LANES)
            acc = cb_ref[:, cs]
            for j in range(CONV_W - 1):
                lo = HALO - (CONV_W - 1) + j
                acc = acc + cw_ref[j:j + 1, cs] * up_scr[slot, s, lo:lo + rows, :]
            parts.append(acc + cw_ref[CONV_W - 1:CONV_W, cs]
                         * up[HALO:, s * V7X_LANES:(s + 1) * V7X_LANES])
        return jnp.concatenate(parts, axis=1)

    for j in range(N_FF_CHUNKS):
        c_gate = up_and_conv(j * FF_CHUNK, 0)
        c_val = up_and_conv(D_FF + j * FF_CHUNK, 1)
        h_scr[:, j * FF_CHUNK:(j + 1) * FF_CHUNK] = (jax.nn.silu(c_gate) * c_val).astype(BF16)
    out_ref[...] = x + jnp.dot(h_scr[...], wdown_ref[...], preferred_element_type=F32)


def _ffn_prompt(layer, x, p):
    rows = TOK_TILE
    n_tok = x.shape[0]
    nt = TILES_PER_SEQ
    halo_per_tile = rows // HALO
    in_specs = [
        pl.BlockSpec((rows, D_MODEL), lambda i: (i, 0)),
        pl.BlockSpec((HALO, D_MODEL), lambda i: (jnp.maximum(i * halo_per_tile - 1, 0), 0)),
        _layer_block((1, D_MODEL), layer),
        _layer_block((D_MODEL, 2 * D_FF), layer),
        _layer_block((CONV_W, 2 * D_FF), layer),
        _layer_block((1, 2 * D_FF), layer),
        _layer_block((D_FF, D_MODEL), layer),
    ]
    return pl.pallas_call(
        _ffn_kernel,
        grid=(n_tok // rows,),
        in_specs=in_specs,
        out_specs=[pl.BlockSpec((rows, D_MODEL), lambda i: (i, 0)),
                   pl.BlockSpec((1, CONV_W - 1, 2 * D_FF), lambda i: (i // nt, 0, 0))],
        out_shape=[jax.ShapeDtypeStruct((n_tok, D_MODEL), F32),
                   jax.ShapeDtypeStruct((BATCH, CONV_W - 1, 2 * D_FF), F32)],
        scratch_shapes=[pltpu.VMEM((HALO + rows, D_MODEL), BF16),
                        pltpu.VMEM((2, FF_CHUNK // V7X_LANES, HALO + rows, V7X_LANES), F32),
                        pltpu.VMEM((rows, D_FF), BF16)],
        compiler_params=_params(),
        name=f"ffn_prompt_l{layer}",
    )(x, x, p["g_ffn"], p["w_up"], p["conv_w"], p["conv_b"], p["w_down"])


def _sample_proj_kernel(x_ref, gmix_ref, w_ref, wg_ref, gv_ref, gq_ref, gk_ref, cos_ref, sin_ref,
                        mean_ref, u_ref, va_ref, q_ref, kv_ref, ga_ref, gb_ref):
    xn = _rms(x_ref[...], gmix_ref[...]).astype(BF16)

    def proj(c0, n):
        return jnp.dot(xn, w_ref[:, c0:c0 + n], preferred_element_type=F32)

    u_ref[...] = jax.nn.gelu(proj(OFF_U, A_WIDTH))
    va_ref[...] = _rms(jax.nn.gelu(proj(OFF_V, A_WIDTH)), gv_ref[...])
    ga_ref[...], gb_ref[...] = _gates(xn, wg_ref)
    for g in range(N_PAIRS):
        q, k, v = _qkv_pair(proj, g, gq_ref[...], gk_ref[...], mean_ref[...], cos_ref[...],
                            sin_ref[...])
        q_ref[:, g * PAIR_WIDTH:(g + 1) * PAIR_WIDTH] = q
        kv_ref[:, 2 * g * PAIR_WIDTH:(2 * g + 1) * PAIR_WIDTH] = k
        kv_ref[:, (2 * g + 1) * PAIR_WIDTH:(2 * g + 2) * PAIR_WIDTH] = v


def _proj_sample(layer, x, p):
    rows = x.shape[0]
    in_specs = [
        _full((rows, D_MODEL)),
        _layer_block((1, D_MODEL), layer),
        _layer_block((D_MODEL, MAIN_COLS), layer),
        _layer_block((D_MODEL, 2 * D_MODEL), layer),
        _layer_block((1, A_WIDTH), layer),
        _layer_block((1, PAIR_WIDTH), layer),
        _layer_block((1, PAIR_WIDTH), layer),
        _full((rows, PAIR_WIDTH)),
        _full((rows, PAIR_WIDTH)),
        _full((PAIR_WIDTH, PAIR_WIDTH)),
    ]
    widths = [A_WIDTH, A_WIDTH, B_WIDTH, 2 * B_WIDTH, D_MODEL, D_MODEL]
    return pl.pallas_call(
        _sample_proj_kernel,
        grid=(1,),
        in_specs=in_specs,
        out_specs=[_full((rows, w)) for w in widths],
        out_shape=[jax.ShapeDtypeStruct((rows, w), F32) for w in widths],
        compiler_params=_params(),
        name=f"proj_sample_l{layer}",
    )(x, p["g_mix"], p["w_main"], p["w_gates"], p["g_v"], p["g_q"], p["g_k"],
      p["cos_s"], p["sin_s"], p["mean_mat"])


def _sample_attn_kernel(qt_ref, kvt_ref, c0_ref, c1_ref, c2_ref, sel1_ref, sel2_ref,
                        out_ref, o_scr, l_scr):
    kv_rows = 2 * HEADS_PER_PAIR * HEAD_DIM
    for g, (c_ref, sel_ref) in enumerate(((c0_ref, None), (c1_ref, sel1_ref),
                                          (c2_ref, sel2_ref))):
        window, dil = DILATED_PAIRS[g]
        for b in range(SAMPLE_GROUP):
            col = slice(b, b + 1)
            if sel_ref is not None:
                blk = c_ref[0, b].reshape(kv_rows, window).astype(BF16)
                taps = jnp.dot(blk, sel_ref[...], preferred_element_type=F32)
            for h in range(HEADS_PER_PAIR):
                if sel_ref is None:
                    kt, vt = c_ref[0, b, 0, h], c_ref[0, b, 1, h]
                else:
                    kt = taps[h * HEAD_DIM:(h + 1) * HEAD_DIM]
                    vt = taps[(HEADS_PER_PAIR + h) * HEAD_DIM:(HEADS_PER_PAIR + h + 1) * HEAD_DIM]
                feat = slice(g * PAIR_WIDTH + h * HEAD_DIM, g * PAIR_WIDTH + (h + 1) * HEAD_DIM)
                kfeat = slice(2 * g * PAIR_WIDTH + h * HEAD_DIM,
                              2 * g * PAIR_WIDTH + (h + 1) * HEAD_DIM)
                vfeat = slice((2 * g + 1) * PAIR_WIDTH + h * HEAD_DIM,
                              (2 * g + 1) * PAIR_WIDTH + (h + 1) * HEAD_DIM)
                q = qt_ref[0, feat, col]
                k_new = kvt_ref[0, kfeat, col]
                v_new = kvt_ref[0, vfeat, col]
                s = jnp.sum(kt * q, axis=0, keepdims=True)
                s_new = jnp.sum(k_new * q, axis=0, keepdims=True)
                mx = jnp.maximum(jnp.max(s, axis=1, keepdims=True), s_new)
                pr = jnp.exp(s - mx)
                p_new = jnp.exp(s_new - mx)
                den = jnp.sum(pr, axis=1, keepdims=True) + p_new
                o = (jnp.sum(vt * pr, axis=1, keepdims=True) + p_new * v_new) / den
                out_rows = slice(h * HEAD_DIM, (h + 1) * HEAD_DIM)
                o_scr[g, out_rows, col] = o
                l_scr[g, out_rows, col] = jnp.broadcast_to(mx + jnp.log(den), (HEAD_DIM, 1))
    l0, l1, l2 = l_scr[0], l_scr[1], l_scr[2]
    m = jnp.maximum(jnp.maximum(l0, l1), l2)
    e0, e1, e2 = jnp.exp(l0 - m), jnp.exp(l1 - m), jnp.exp(l2 - m)
    den = e0 + e1 + e2
    out_ref[0] = (e0 / den) * o_scr[0] + (e1 / den) * o_scr[1] + (e2 / den) * o_scr[2]


def _attn_sample(layer, qt, kvt, caches, tap_select):
    n = SAMPLE_GROUP
    groups = qt.shape[0]
    in_specs = [pl.BlockSpec((1, B_WIDTH, n), lambda i: (i, 0, 0)),
                pl.BlockSpec((1, 2 * B_WIDTH, n), lambda i: (i, 0, 0))]
    for (window, dil) in DILATED_PAIRS:
        in_specs.append(pl.BlockSpec((1, n, 2, HEADS_PER_PAIR, HEAD_DIM, window),
                                     lambda i: (layer, i, 0, 0, 0, 0)))
    in_specs += [_full(sel.shape) for sel in tap_select]
    scr = pltpu.VMEM((N_PAIRS, PAIR_WIDTH, n), F32)
    return pl.pallas_call(
        _sample_attn_kernel,
        grid=(groups,),
        in_specs=in_specs,
        out_specs=pl.BlockSpec((1, PAIR_WIDTH, n), lambda i: (i, 0, 0)),
        out_shape=jax.ShapeDtypeStruct((groups, PAIR_WIDTH, n), F32),
        scratch_shapes=[scr, scr],
        compiler_params=_params(),
        name=f"attn_sample_l{layer}",
    )(qt, kvt, *caches, *tap_select)


def _sample_mix_kernel(x_ref, u_ref, va_ref, bo_ref, ga_ref, gb_ref, wdiag_ref, bias0_ref,
                       wa_ref, wb_ref, wo_ref, out_ref):
    a_out = u_ref[...] * (wdiag_ref[...] * va_ref[...] + bias0_ref[...])
    out_ref[...] = _gated_merge(x_ref[...], a_out, bo_ref[...], ga_ref[...], gb_ref[...],
                                wa_ref, wb_ref, wo_ref)


def _mix_sample(layer, x, u, va, b_out, ga, gb, p):
    rows = x.shape[0]
    in_specs = [_full((rows, D_MODEL)), _full((rows, A_WIDTH)), _full((rows, A_WIDTH)),
                _full((rows, PAIR_WIDTH)), _full((rows, D_MODEL)), _full((rows, D_MODEL)),
                _layer_block((1, A_WIDTH), layer), _layer_block((1, A_WIDTH), layer),
                _layer_block((A_WIDTH, D_MODEL), layer),
                _layer_block((PAIR_WIDTH, D_MODEL), layer),
                _layer_block((D_MODEL, D_MODEL), layer)]
    return pl.pallas_call(
        _sample_mix_kernel,
        grid=(1,),
        in_specs=in_specs,
        out_specs=_full((rows, D_MODEL)),
        out_shape=jax.ShapeDtypeStruct((rows, D_MODEL), F32),
        compiler_params=_params(),
        name=f"mix_sample_l{layer}",
    )(x, u, va, b_out, ga, gb, p["wdiag"], p["bias0"], p["w_a"], p["w_b"], p["w_o"])


def _sample_ffn_kernel(x_ref, g_ref, wg_ref, wv_ref, cwg_ref, cwv_ref, cbg_ref, cbv_ref,
                       h0g_ref, h0v_ref, h1g_ref, h1v_ref, wdown_ref,
                       out_ref, upg_ref, upv_ref):
    j = pl.program_id(0)
    x = x_ref[...]
    xn = _rms(x, g_ref[...]).astype(BF16)

    def up_and_conv(w_ref, cw_ref, cb_ref, h0_ref, h1_ref, up_ref):
        up = jnp.dot(xn, w_ref[...], preferred_element_type=F32)
        up_ref[...] = up
        return (cb_ref[...] + cw_ref[0:1, :] * h0_ref[...] + cw_ref[1:2, :] * h1_ref[...]
                + cw_ref[2:3, :] * up)

    c_gate = up_and_conv(wg_ref, cwg_ref, cbg_ref, h0g_ref, h1g_ref, upg_ref)
    c_val = up_and_conv(wv_ref, cwv_ref, cbv_ref, h0v_ref, h1v_ref, upv_ref)
    h = (jax.nn.silu(c_gate) * c_val).astype(BF16)
    part = jnp.dot(h, wdown_ref[...], preferred_element_type=F32)

    @pl.when(j == 0)
    def _():
        out_ref[...] = x + part

    @pl.when(j > 0)
    def _():
        out_ref[...] += part


def _ffn_sample(layer, x, p, state_conv):
    rows = x.shape[0]
    nch = N_FF_CHUNKS
    fc = FF_CHUNK

    def cols(nrows, blk):
        return pl.BlockSpec((None, nrows, fc), lambda j: (layer, 0, blk * nch + j))

    state_flat = state_conv.reshape(DEPTH, rows, (CONV_W - 1) * 2 * D_FF)
    hist = [cols(rows, 2 * row + half) for row in range(CONV_W - 1) for half in range(2)]
    in_specs = [
        _full((rows, D_MODEL)),
        _layer_block((1, D_MODEL), layer),
        cols(D_MODEL, 0), cols(D_MODEL, 1),
        cols(CONV_W, 0), cols(CONV_W, 1),
        cols(1, 0), cols(1, 1),
        *hist,
        pl.BlockSpec((None, fc, D_MODEL), lambda j: (layer, j, 0)),
    ]
    up_spec = pl.BlockSpec((rows, fc), lambda j: (0, j))
    up_shape = jax.ShapeDtypeStruct((rows, D_FF), F32)
    return pl.pallas_call(
        _sample_ffn_kernel,
        grid=(nch,),
        in_specs=in_specs,
        out_specs=[_full((rows, D_MODEL)), up_spec, up_spec],
        out_shape=[jax.ShapeDtypeStruct((rows, D_MODEL), F32), up_shape, up_shape],
        compiler_params=_params(),
        name=f"ffn_sample_l{layer}",
    )(x, p["g_ffn"], p["w_up"], p["w_up"], p["conv_w"], p["conv_w"], p["conv_b"], p["conv_b"],
      state_flat, state_flat, state_flat, state_flat, p["w_down"])


def _sample_layer(layer, xs, p, caches_t, state_conv):
    n = SAMPLE_GROUP
    rows = xs.shape[0]
    u, va, q, kv, ga, gb = _proj_sample(layer, xs, p)

    def columns(a):
        return a.reshape(rows // n, n, a.shape[1]).swapaxes(1, 2)

    bt = _attn_sample(layer, columns(q), columns(kv), caches_t, p["tap_select"])
    b_out = bt.swapaxes(1, 2).reshape(rows, PAIR_WIDTH)
    xs = _mix_sample(layer, xs, u, va, b_out, ga, gb, p)
    xs, upg, upv = _ffn_sample(layer, xs, p, state_conv)
    up_row = jnp.concatenate([upg, upv], axis=-1)[:, None, :]
    new_conv = jnp.concatenate([state_conv[layer][:, 1:], up_row], axis=1)
    return xs, kv, va, new_conv


def _rope_tables(pos):
    half = HEAD_DIM // 2
    inv_freq = ROPE_THETA ** (-jnp.arange(half, dtype=F32) / half)
    ang = pos.astype(F32)[:, None] * inv_freq[None, :]
    cos, sin = jnp.cos(ang), jnp.sin(ang)
    cos_full = jnp.tile(jnp.concatenate([cos, cos], axis=1), (1, HEADS_PER_PAIR))
    sin_signed = jnp.tile(jnp.concatenate([-sin, sin], axis=1), (1, HEADS_PER_PAIR))
    return cos_full, sin_signed


def _prepare_params(g_mix, w_in, g_v, w_s, b_s, g_q, g_k, w_a_proj, w_b_proj, w_o,
                    g_ffn, w_up, conv_w, conv_b, w_down):
    head_of = jnp.arange(PAIR_WIDTH) // HEAD_DIM
    same_head = head_of[:, None] == head_of[None, :]
    bias_full = jnp.repeat(jnp.swapaxes(b_s, 1, 2), A_GROUP_DIM, axis=2)
    cos_p, sin_p = _rope_tables(jnp.arange(SEQ))
    cos_s, sin_s = _rope_tables(jnp.full((DEC_BATCH,), PAST_LEN))
    tap_select = [
        (jnp.arange(window)[:, None] == jnp.arange(TAPS)[None, :] * dil).astype(BF16)
        for (window, dil) in DILATED_PAIRS if dil > 1]
    return dict(
        tap_select=tap_select,
        g_mix=g_mix[:, None, :], g_v=g_v[:, None, :], g_ffn=g_ffn[:, None, :],
        g_q=jnp.tile(g_q, (1, HEADS_PER_PAIR))[:, None, :],
        g_k=jnp.tile(g_k, (1, HEADS_PER_PAIR))[:, None, :],
        w_main=w_in[:, :, :MAIN_COLS].astype(BF16),
        w_gates=w_in[:, :, OFF_GATES:].astype(BF16),
        w_a=w_a_proj.astype(BF16), w_b=w_b_proj.astype(BF16), w_o=w_o.astype(BF16),
        w_up=w_up.astype(BF16), w_down=w_down.astype(BF16),
        w_s=w_s, bias_full=bias_full, bias0=bias_full[:, 0:1, :],
        wdiag=jnp.repeat(w_s[:, :, 0, 0], A_GROUP_DIM, axis=1)[:, None, :],
        conv_w=conv_w, conv_b=conv_b[:, None, :],
        mean_mat=jnp.where(same_head, 1.0 / HEAD_DIM, 0.0).astype(BF16),
        cos_p=cos_p, sin_p=sin_p, cos_s=cos_s, sin_s=sin_s,
    )


def kernel(x_prompt, x_sample, cache_kv_w128, cache_kv_w512, cache_kv_w2048, state_conv,
           g_mix, w_in, g_v, w_s, b_s, g_q, g_k, w_a_proj, w_b_proj, w_o,
           g_ffn, w_up, conv_w, conv_b, w_down):
    assert x_prompt.shape == (BATCH, SEQ, D_MODEL) and x_sample.shape == (DEC_BATCH, 1, D_MODEL)
    p = _prepare_params(g_mix, w_in, g_v, w_s, b_s, g_q, g_k, w_a_proj, w_b_proj, w_o,
                        g_ffn, w_up, conv_w, conv_b, w_down)
    caches_t = []
    for cache, (window, dil) in zip((cache_kv_w128, cache_kv_w512, cache_kv_w2048), DILATED_PAIRS):
        assert cache.shape[2] == window
        caches_t.append(jnp.transpose(cache, (0, 1, 3, 4, 5, 2)))

    xp = x_prompt.reshape(BATCH * SEQ, D_MODEL)
    xs = x_sample.reshape(DEC_BATCH, D_MODEL)
    kv_prompt = None
    conv_p, conv_s, kv_s, va_s = [], [], [], []
    for l in range(DEPTH):
        outs = _proj_prompt(l, xp, p, kv_prompt)
        u, va = outs[0], outs[1]
        qkv = outs[2:11]
        kv_prompt = outs[11:14]
        b_out = _attn_prompt(l, qkv)
        xp = _mix_prompt(l, xp, u, va, b_out, p)
        xp, nconv = _ffn_prompt(l, xp, p)
        conv_p.append(nconv)

        xs, skv, sva, sconv = _sample_layer(l, xs, p, caches_t, state_conv)
        kv_s.append(skv)
        va_s.append(sva)
        conv_s.append(sconv)

    def kv_shape(a, rows):
        return a.reshape(DEPTH, -1, rows, 2, HEADS_PER_PAIR, HEAD_DIM)

    kv_s = jnp.stack(kv_s, axis=0)
    kv_s = [kv_s[:, :, 2 * g * PAIR_WIDTH:(2 * g + 2) * PAIR_WIDTH] for g in range(N_PAIRS)]
    return (
        xp.reshape(BATCH, SEQ, D_MODEL),
        xs.reshape(DEC_BATCH, 1, D_MODEL),
        kv_shape(kv_prompt[0], min(DILATED_PAIRS[0][0], SEQ)),
        kv_shape(kv_prompt[1], min(DILATED_PAIRS[1][0], SEQ)),
        kv_shape(kv_prompt[2], min(DILATED_PAIRS[2][0], SEQ)),
        jnp.stack(conv_p, axis=0),
        kv_shape(kv_s[0], 1),
        kv_shape(kv_s[1], 1),
        kv_shape(kv_s[2], 1),
        jnp.stack(conv_s, axis=0),
        jnp.stack(va_s, axis=0).reshape(DEPTH, DEC_BATCH, 1, A_WIDTH),
    )
```

```python
import functools

import jax
import jax.numpy as jnp
from jax import lax
from jax.experimental import pallas as pl
from jax.experimental.pallas import tpu as pltpu

D_MODEL = 1024
BATCH = 16
SEQ = 2048
DEPTH = 2
DEC_BATCH = 32
PAST_LEN = 16384
CHUNK = 128
A_GROUPS = 8
A_GROUP_DIM = 64
A_WIDTH = A_GROUPS * A_GROUP_DIM
HEAD_DIM = 64
HEADS_PER_PAIR = 4
DILATED_PAIRS = ((128, 1), (512, 4), (2048, 16))
N_PAIRS = len(DILATED_PAIRS)
TAPS = DILATED_PAIRS[0][0] // DILATED_PAIRS[0][1]
PAIR_WIDTH = HEADS_PER_PAIR * HEAD_DIM
B_WIDTH = N_PAIRS * PAIR_WIDTH
QBLK = 128
D_FF = 2816
CONV_W = 3
ROPE_THETA = 10000.0
EPS = 1e-6
NEG_INF = -1e30
LOG2E = 1.4426950408889634
LN2 = 0.6931471805599453
OFF_U, OFF_V = 0, A_WIDTH
OFF_Q = 2 * A_WIDTH
OFF_K = OFF_Q + B_WIDTH
OFF_VV = OFF_K + B_WIDTH
OFF_GATES = OFF_VV + B_WIDTH
MAIN_COLS = OFF_GATES

V7X_LANES = 128
V7X_VMEM_BYTES = 64 * 1024 * 1024
VMEM_LIMIT = V7X_VMEM_BYTES - 8 * 1024 * 1024

TOK_TILE = 512
TILES_PER_SEQ = SEQ // TOK_TILE
FFN_TILE = 1024
PROJ_TILE = 512
HALO = 16
FF_CHUNK = 256
N_FF_CHUNKS = D_FF // FF_CHUNK
ATTN_UNROLL = 16
SAMPLE_GROUP = 2

F32 = jnp.float32
BF16 = jnp.bfloat16

assert all(w // d == TAPS for (w, d) in DILATED_PAIRS) and TAPS >= QBLK - 1


def _params(**kw):
    return pltpu.CompilerParams(dimension_semantics=("arbitrary",),
                                vmem_limit_bytes=VMEM_LIMIT, **kw)


def _rms(x, g):
    return x * lax.rsqrt(jnp.mean(x * x, axis=-1, keepdims=True) + EPS) * g


def _swap_halves(t):
    lane = lax.broadcasted_iota(jnp.int32, (1, V7X_LANES), 1)
    first_half = (lane % HEAD_DIM) < (HEAD_DIM // 2)
    slabs = []
    for j in range(t.shape[1] // V7X_LANES):
        s = t[:, j * V7X_LANES:(j + 1) * V7X_LANES]
        up = pltpu.roll(s, V7X_LANES - HEAD_DIM // 2, 1)
        down = pltpu.roll(s, HEAD_DIM // 2, 1)
        slabs.append(jnp.where(first_half, up, down))
    return jnp.concatenate(slabs, axis=1)


def _head_norm_rope(t, gain, mean_mat, cos, sin_signed):
    ms = jnp.dot((t * t).astype(BF16), mean_mat, preferred_element_type=F32)
    tn = t * lax.rsqrt(ms + EPS) * gain
    return tn * cos + _swap_halves(tn) * sin_signed


def _full(shape):
    return pl.BlockSpec(shape, lambda i: (0,) * len(shape))


def _layer_block(shape, layer):
    return pl.BlockSpec((None,) + tuple(shape), lambda i: (layer,) + (0,) * len(shape))


def _qkv_pair(proj, g, gq, gk, mean_mat, cos, sin_signed, q_scale=HEAD_DIM ** -0.5):
    c = g * PAIR_WIDTH
    q = _head_norm_rope(proj(OFF_Q + c, PAIR_WIDTH), gq, mean_mat, cos, sin_signed)
    q = q * q_scale
    k = _head_norm_rope(proj(OFF_K + c, PAIR_WIDTH), gk, mean_mat, cos, sin_signed)
    v = proj(OFF_VV + c, PAIR_WIDTH)
    return q, k, v


def _proj_kernel(x_ref, gmix_ref, w_ref, gv_ref, gq_ref, gk_ref, cos_ref, sin_ref, mean_ref,
                 *refs, n_prev):
    prev_kv_refs = refs[:N_PAIRS] if n_prev else (None,) * N_PAIRS
    (u_ref, va_ref, q0_ref, k0_ref, v0_ref, q1_ref, k1_ref, v1_ref, q2_ref, k2_ref, v2_ref,
     kv0_ref, kv1_ref, kv2_ref, scr_ref) = refs[N_PAIRS if n_prev else 0:]
    rows = PROJ_TILE
    qkv_refs = ((q0_ref, k0_ref, v0_ref), (q1_ref, k1_ref, v1_ref), (q2_ref, k2_ref, v2_ref))
    kv_refs = (kv0_ref, kv1_ref, kv2_ref)
    xn = _rms(x_ref[...], gmix_ref[...]).astype(BF16)

    def proj(c0, n):
        return jnp.dot(xn, w_ref[:, c0:c0 + n], preferred_element_type=F32)

    cos = cos_ref[...]
    sin_signed = sin_ref[...]
    mean_mat = mean_ref[...]
    n_slabs = PAIR_WIDTH // V7X_LANES

    def store_residue_major(dst_ref, slot, val, dil):
        for j in range(n_slabs):
            scr_ref[slot, j] = val[:, j * V7X_LANES:(j + 1) * V7X_LANES]
        for r in range(dil):
            picked = [scr_ref[slot, j, pl.ds(r, rows // dil, stride=dil), :]
                      for j in range(n_slabs)]
            dst_ref[0, r] = jnp.concatenate(picked, axis=1).astype(dst_ref.dtype)

    slot = 0
    for g, (window, dil) in reversed(list(enumerate(DILATED_PAIRS))):
        q, k, v = _qkv_pair(proj, g, gq_ref[...], gk_ref[...], mean_mat, cos, sin_signed,
                            q_scale=HEAD_DIM ** -0.5 * LOG2E)
        keep = min(min(window, SEQ), rows)
        if n_prev:
            kv_refs[g][:n_prev] = prev_kv_refs[g][...]
        kv_refs[g][n_prev, 0, :, :PAIR_WIDTH] = k[rows - keep:]
        kv_refs[g][n_prev, 0, :, PAIR_WIDTH:] = v[rows - keep:]
        for ref, val in zip(qkv_refs[g], (q, k, v)):
            if dil == 1:
                ref[...] = val.astype(ref.dtype)
            else:
                store_residue_major(ref, slot, val, dil)
                slot += 1
    va_ref[...] = _rms(jax.nn.gelu(proj(OFF_V, A_WIDTH)), gv_ref[...]).astype(va_ref.dtype)
    u_ref[...] = jax.nn.gelu(proj(OFF_U, A_WIDTH)).astype(u_ref.dtype)


def _proj_prompt(layer, x, p, kv_prev):
    rows = PROJ_TILE
    n_tok = x.shape[0]
    nt = SEQ // rows

    def tok(width):
        return pl.BlockSpec((rows, width), lambda i: (i, 0))

    in_specs = [
        tok(D_MODEL),
        _layer_block((1, D_MODEL), layer),
        _layer_block((D_MODEL, MAIN_COLS), layer),
        _layer_block((1, A_WIDTH), layer),
        _layer_block((1, PAIR_WIDTH), layer),
        _layer_block((1, PAIR_WIDTH), layer),
        pl.BlockSpec((rows, PAIR_WIDTH), lambda i: (i % nt, 0)),
        pl.BlockSpec((rows, PAIR_WIDTH), lambda i: (i % nt, 0)),
        _full((PAIR_WIDTH, PAIR_WIDTH)),
    ]
    out_shape = [jax.ShapeDtypeStruct((n_tok, A_WIDTH), BF16)] * 2
    out_specs = [tok(A_WIDTH)] * 2
    n_strided = 0
    for (window, dil) in DILATED_PAIRS:
        if dil == 1:
            out_shape += [jax.ShapeDtypeStruct((n_tok, PAIR_WIDTH), BF16)] * 3
            out_specs += [tok(PAIR_WIDTH)] * 3
        else:
            n_strided += 3
            out_shape += [jax.ShapeDtypeStruct((BATCH, dil, SEQ // dil, PAIR_WIDTH), BF16)] * 3
            out_specs += [pl.BlockSpec((1, dil, rows // dil, PAIR_WIDTH),
                                       lambda i: (i // nt, 0, i % nt, 0))] * 3
    args = [x, p["g_mix"], p["w_main"], p["g_v"], p["g_q"], p["g_k"], p["cos_p"], p["sin_p"],
            p["mean_mat"]]
    for g, (window, dil) in enumerate(DILATED_PAIRS):
        keep = min(window, SEQ)
        first_tile = (SEQ - max(keep, rows)) // rows

        def kv_spec(n_layers, ft=first_tile, blk=min(keep, rows)):
            return pl.BlockSpec((n_layers, 1, blk, 2 * PAIR_WIDTH),
                                lambda i: (0, i // nt, jnp.maximum(i % nt - ft, 0), 0))

        out_shape.append(jax.ShapeDtypeStruct((layer + 1, BATCH, keep, 2 * PAIR_WIDTH), F32))
        out_specs.append(kv_spec(layer + 1))
        if layer > 0:
            in_specs.append(kv_spec(layer))
            args.append(kv_prev[g])

    return pl.pallas_call(
        functools.partial(_proj_kernel, n_prev=layer),
        grid=(n_tok // rows,),
        in_specs=in_specs,
        out_specs=out_specs,
        out_shape=out_shape,
        scratch_shapes=[pltpu.VMEM((n_strided, PAIR_WIDTH // V7X_LANES, rows, V7X_LANES), F32)],
        compiler_params=_params(),
        name=f"proj_prompt_l{layer}",
    )(*args)


def _attn_block(q, kk, vv, valid):
    nk = kk.shape[0]
    lane_head = lax.broadcasted_iota(jnp.int32, (1, PAIR_WIDTH), 1) // HEAD_DIM
    head_masks = [lane_head == h for h in range(HEADS_PER_PAIR)]
    qs = jnp.concatenate([jnp.where(hm, q, jnp.zeros_like(q)) for hm in head_masks], axis=0)
    s = lax.dot_general(qs, kk, (((1,), (1,)), ((), ())), preferred_element_type=F32)
    s = jnp.where(valid[None], s.reshape(HEADS_PER_PAIR, QBLK, nk), NEG_INF)
    mx = jnp.max(s, axis=-1, keepdims=True)
    p = jnp.exp2(s - mx)
    den = jnp.sum(p, axis=-1, keepdims=True)
    pv = jnp.dot(p.reshape(HEADS_PER_PAIR * QBLK, nk).astype(BF16), vv,
                 preferred_element_type=F32).reshape(HEADS_PER_PAIR, QBLK, PAIR_WIDTH)
    inv_den = 1.0 / den
    lse_h = mx * LN2 + jnp.log(den)
    o, inv, lse = pv[0], inv_den[0], lse_h[0]
    for h in range(1, HEADS_PER_PAIR):
        o = jnp.where(head_masks[h], pv[h], o)
        inv = jnp.where(head_masks[h], inv_den[h], inv)
        lse = jnp.where(head_masks[h], lse_h[h], lse)
    return o * inv, jnp.broadcast_to(lse, (QBLK, PAIR_WIDTH))


def _attn_kernel(q0_ref, k0_ref, v0_ref, q1_ref, k1_ref, v1_ref, q2_ref, k2_ref, v2_ref,
                 out_ref, o_scr, l_scr):
    qi = lax.broadcasted_iota(jnp.int32, (QBLK, QBLK), 0)
    ki = lax.broadcasted_iota(jnp.int32, (QBLK, QBLK), 1)
    causal = ki <= qi
    qi2 = lax.broadcasted_iota(jnp.int32, (QBLK, 2 * QBLK), 0)
    ki2 = lax.broadcasted_iota(jnp.int32, (QBLK, 2 * QBLK), 1)
    dist = qi2 - ki2 + QBLK
    band = (dist >= 0) & (dist <= TAPS)
    n_slabs = PAIR_WIDTH // V7X_LANES

    def put(scr, g, rows, val):
        for j in range(n_slabs):
            scr[g, j, rows, :] = val[:, j * V7X_LANES:(j + 1) * V7X_LANES]

    def get(scr, g, rows):
        return jnp.concatenate([scr[g, j, rows, :] for j in range(n_slabs)], axis=1)

    for g, refs in enumerate(((q0_ref, k0_ref, v0_ref), (q1_ref, k1_ref, v1_ref),
                              (q2_ref, k2_ref, v2_ref))):
        dil = DILATED_PAIRS[g][1]
        n_blocks = (SEQ // dil) // QBLK

        def rows_of(ref, r, start, dil=dil):
            if dil == 1:
                return ref[pl.ds(start, QBLK), :]
            return ref[0, r, pl.ds(start, QBLK), :]

        def step(idx, carry, g=g, dil=dil, n_blocks=n_blocks, refs=refs, rows_of=rows_of):
            q_ref, k_ref, v_ref = refs
            r, i = idx // n_blocks, idx % n_blocks
            cur = pl.multiple_of(i * QBLK, QBLK)
            q = rows_of(q_ref, r, cur)
            if n_blocks == 1:
                o, lse = _attn_block(q, rows_of(k_ref, r, cur), rows_of(v_ref, r, cur), causal)
            else:
                prev = pl.multiple_of(jnp.maximum(i - 1, 0) * QBLK, QBLK)
                kk = jnp.concatenate([rows_of(k_ref, r, prev), rows_of(k_ref, r, cur)], axis=0)
                vv = jnp.concatenate([rows_of(v_ref, r, prev), rows_of(v_ref, r, cur)], axis=0)
                first_key = jnp.where(i > 0, 0, QBLK)
                o, lse = _attn_block(q, kk, vv, band & (ki2 >= first_key))
            if dil == 1:
                rows = pl.ds(cur, QBLK)
            else:
                rows = pl.ds(r + i * (QBLK * dil), QBLK, stride=dil)
            put(o_scr, g, rows, o)
            put(l_scr, g, rows, lse)
            return carry

        lax.fori_loop(0, dil * n_blocks, step, 0, unroll=ATTN_UNROLL)

    step_rows = 256
    for t0 in range(0, SEQ, step_rows):
        rows = slice(t0, t0 + step_rows)
        l0, l1, l2 = get(l_scr, 0, rows), get(l_scr, 1, rows), get(l_scr, 2, rows)
        m = jnp.maximum(jnp.maximum(l0, l1), l2)
        e0, e1, e2 = jnp.exp(l0 - m), jnp.exp(l1 - m), jnp.exp(l2 - m)
        den = e0 + e1 + e2
        merged = ((e0 / den) * get(o_scr, 0, rows) + (e1 / den) * get(o_scr, 1, rows)
                  + (e2 / den) * get(o_scr, 2, rows))
        out_ref[rows, :] = merged.astype(out_ref.dtype)


def _attn_prompt(layer, qkv):
    n_tok = qkv[0].shape[0]
    in_specs = []
    for (window, dil) in DILATED_PAIRS:
        if dil == 1:
            in_specs += [pl.BlockSpec((SEQ, PAIR_WIDTH), lambda b: (b, 0))] * 3
        else:
            in_specs += [pl.BlockSpec((1, dil, SEQ // dil, PAIR_WIDTH),
                                      lambda b: (b, 0, 0, 0))] * 3
    scr = pltpu.VMEM((N_PAIRS, PAIR_WIDTH // V7X_LANES, SEQ, V7X_LANES), F32)
    return pl.pallas_call(
        _attn_kernel,
        grid=(BATCH,),
        in_specs=in_specs,
        out_specs=pl.BlockSpec((SEQ, PAIR_WIDTH), lambda b: (b, 0)),
        out_shape=jax.ShapeDtypeStruct((n_tok, PAIR_WIDTH), BF16),
        scratch_shapes=[scr, scr],
        compiler_params=_params(),
        name=f"attn_prompt_l{layer}",
    )(*qkv)


def _gated_merge(x, a_out, b_out, ga, gb, wa_ref, wb_ref, wo_ref):
    ap = jnp.dot(a_out.astype(BF16), wa_ref[...], preferred_element_type=F32)
    bp = jnp.dot(b_out.astype(BF16), wb_ref[...], preferred_element_type=F32)
    merged = ga * ap + gb * bp
    return x + jnp.dot(merged.astype(BF16), wo_ref[...], preferred_element_type=F32)


def _gates(xn, wg_ref):
    ga = jax.nn.sigmoid(jnp.dot(xn, wg_ref[:, :D_MODEL], preferred_element_type=F32))
    gb = jax.nn.sigmoid(jnp.dot(xn, wg_ref[:, D_MODEL:], preferred_element_type=F32))
    return ga, gb


def _mix_kernel(x_ref, u_ref, va_ref, bo_ref, gmix_ref, wg_ref, ws_ref, bias_ref,
                wa_ref, wb_ref, wo_ref, out_ref, a_scr):
    ti = lax.broadcasted_iota(jnp.int32, (CHUNK, CHUNK), 0)
    si = lax.broadcasted_iota(jnp.int32, (CHUNK, CHUNK), 1)
    tril = si <= ti
    w_tril = [jnp.where(tril, ws_ref[g], 0.0).astype(BF16) for g in range(A_GROUPS)]
    low_group = lax.broadcasted_iota(jnp.int32, (1, V7X_LANES), 1) < A_GROUP_DIM
    bias = bias_ref[...]
    groups_per_slab = V7X_LANES // A_GROUP_DIM
    for c in range(TOK_TILE // CHUNK):
        rows = slice(c * CHUNK, (c + 1) * CHUNK)
        zs = []
        for j in range(A_WIDTH // V7X_LANES):
            slab = va_ref[rows, j * V7X_LANES:(j + 1) * V7X_LANES]
            z_lo = jnp.dot(w_tril[groups_per_slab * j], slab, preferred_element_type=F32)
            z_hi = jnp.dot(w_tril[groups_per_slab * j + 1], slab, preferred_element_type=F32)
            zs.append(jnp.where(low_group, z_lo, z_hi))
        z = jnp.concatenate(zs, axis=1) + bias
        a_scr[rows, :] = (u_ref[rows, :].astype(F32) * z).astype(BF16)
    x = x_ref[...]
    ga, gb = _gates(_rms(x, gmix_ref[...]).astype(BF16), wg_ref)
    out_ref[...] = _gated_merge(x, a_scr[...], bo_ref[...], ga, gb, wa_ref, wb_ref, wo_ref)


def _mix_prompt(layer, x, u, va, b_out, p):
    rows = TOK_TILE
    n_tok = x.shape[0]

    def tok(width):
        return pl.BlockSpec((rows, width), lambda i: (i, 0))

    in_specs = [
        tok(D_MODEL), tok(A_WIDTH), tok(A_WIDTH), tok(PAIR_WIDTH),
        _layer_block((1, D_MODEL), layer),
        _layer_block((D_MODEL, 2 * D_MODEL), layer),
        _layer_block((A_GROUPS, CHUNK, CHUNK), layer),
        _layer_block((CHUNK, A_WIDTH), layer),
        _layer_block((A_WIDTH, D_MODEL), layer),
        _layer_block((PAIR_WIDTH, D_MODEL), layer),
        _layer_block((D_MODEL, D_MODEL), layer),
    ]
    return pl.pallas_call(
        _mix_kernel,
        grid=(n_tok // rows,),
        in_specs=in_specs,
        out_specs=tok(D_MODEL),
        out_shape=jax.ShapeDtypeStruct((n_tok, D_MODEL), F32),
        scratch_shapes=[pltpu.VMEM((rows, A_WIDTH), BF16)],
        compiler_params=_params(),
        name=f"mix_prompt_l{layer}",
    )(x, u, va, b_out, p["g_mix"], p["w_gates"], p["w_s"], p["bias_full"],
      p["w_a"], p["w_b"], p["w_o"])


def _ffn_kernel(x_ref, halo_ref, g_ref, wup_ref, cw_ref, cb_ref, wdown_ref,
                out_ref, conv_ref, xn_scr, up_scr, h_scr):
    rows = FFN_TILE
    tile = pl.program_id(0) % (SEQ // FFN_TILE)
    x = x_ref[...]
    g = g_ref[...]
    halo_on = jnp.where(tile == 0, 0.0, 1.0)
    xn_scr[:HALO, :] = (_rms(halo_ref[...], g) * halo_on).astype(BF16)
    xn_scr[HALO:, :] = _rms(x, g).astype(BF16)
    xn = xn_scr[...]
    n_slabs = FF_CHUNK // V7X_LANES

    def up_and_conv(c0, slot):
        cols = slice(c0, c0 + FF_CHUNK)
        up = jnp.dot(xn, wup_ref[:, cols], preferred_element_type=F32)
        conv_ref[0, :, cols] = up[HALO + rows - (CONV_W - 1):]
        for s in range(n_slabs):
            up_scr[slot, s] = up[:, s * V7X_LANES:(s + 1) * V7X_LANES]
        parts = []
        for s in range(n_slabs):
            cs = slice(c0 + s * V7X_LANES, c0 + (s + 1) * V7X_LANES)
            acc = cb_ref[:, cs]
            for j in range(CONV_W - 1):
                lo = HALO - (CONV_W - 1) + j
                acc = acc + cw_ref[j:j + 1, cs] * up_scr[slot, s, lo:lo + rows, :]
            parts.append(acc + cw_ref[CONV_W - 1:CONV_W, cs]
                         * up[HALO:, s * V7X_LANES:(s + 1) * V7X_LANES])
        return jnp.concatenate(parts, axis=1)

    for j in range(N_FF_CHUNKS):
        c_gate = up_and_conv(j * FF_CHUNK, 0)
        c_val = up_and_conv(D_FF + j * FF_CHUNK, 1)
        h_scr[:, j * FF_CHUNK:(j + 1) * FF_CHUNK] = (jax.nn.silu(c_gate) * c_val).astype(BF16)
    out_ref[...] = x + jnp.dot(h_scr[...], wdown_ref[...], preferred_element_type=F32)


def _ffn_prompt(layer, x, p):
    rows = FFN_TILE
    n_tok = x.shape[0]
    nt = SEQ // rows
    halo_per_tile = rows // HALO

    def resident(shape):
        return pl.BlockSpec((None,) + tuple(shape), lambda i: (layer,) + (0,) * len(shape),
                            pipeline_mode=pl.Buffered(1))

    in_specs = [
        pl.BlockSpec((rows, D_MODEL), lambda i: (i, 0)),
        pl.BlockSpec((HALO, D_MODEL), lambda i: (jnp.maximum(i * halo_per_tile - 1, 0), 0)),
        _layer_block((1, D_MODEL), layer),
        resident((D_MODEL, 2 * D_FF)),
        _layer_block((CONV_W, 2 * D_FF), layer),
        _layer_block((1, 2 * D_FF), layer),
        resident((D_FF, D_MODEL)),
    ]
    return pl.pallas_call(
        _ffn_kernel,
        grid=(n_tok // rows,),
        in_specs=in_specs,
        out_specs=[pl.BlockSpec((rows, D_MODEL), lambda i: (i, 0)),
                   pl.BlockSpec((1, CONV_W - 1, 2 * D_FF), lambda i: (i // nt, 0, 0))],
        out_shape=[jax.ShapeDtypeStruct((n_tok, D_MODEL), F32),
                   jax.ShapeDtypeStruct((BATCH, CONV_W - 1, 2 * D_FF), F32)],
        scratch_shapes=[pltpu.VMEM((HALO + rows, D_MODEL), BF16),
                        pltpu.VMEM((2, FF_CHUNK // V7X_LANES, HALO + rows, V7X_LANES), F32),
                        pltpu.VMEM((rows, D_FF), BF16)],
        compiler_params=_params(),
        name=f"ffn_prompt_l{layer}",
    )(x, x, p["g_ffn"], p["w_up"], p["conv_w"], p["conv_b"], p["w_down"])


def _sample_proj_kernel(x_ref, gmix_ref, w_ref, wg_ref, gv_ref, gq_ref, gk_ref, cos_ref, sin_ref,
                        mean_ref, u_ref, va_ref, q_ref, kv_ref, ga_ref, gb_ref):
    xn = _rms(x_ref[...], gmix_ref[...]).astype(BF16)

    def proj(c0, n):
        return jnp.dot(xn, w_ref[:, c0:c0 + n], preferred_element_type=F32)

    u_ref[...] = jax.nn.gelu(proj(OFF_U, A_WIDTH))
    va_ref[...] = _rms(jax.nn.gelu(proj(OFF_V, A_WIDTH)), gv_ref[...])
    ga_ref[...], gb_ref[...] = _gates(xn, wg_ref)
    for g in range(N_PAIRS):
        q, k, v = _qkv_pair(proj, g, gq_ref[...], gk_ref[...], mean_ref[...], cos_ref[...],
                            sin_ref[...])
        q_ref[:, g * PAIR_WIDTH:(g + 1) * PAIR_WIDTH] = q
        kv_ref[:, 2 * g * PAIR_WIDTH:(2 * g + 1) * PAIR_WIDTH] = k
        kv_ref[:, (2 * g + 1) * PAIR_WIDTH:(2 * g + 2) * PAIR_WIDTH] = v


def _proj_sample(layer, x, p):
    rows = x.shape[0]
    in_specs = [
        _full((rows, D_MODEL)),
        _layer_block((1, D_MODEL), layer),
        _layer_block((D_MODEL, MAIN_COLS), layer),
        _layer_block((D_MODEL, 2 * D_MODEL), layer),
        _layer_block((1, A_WIDTH), layer),
        _layer_block((1, PAIR_WIDTH), layer),
        _layer_block((1, PAIR_WIDTH), layer),
        _full((rows, PAIR_WIDTH)),
        _full((rows, PAIR_WIDTH)),
        _full((PAIR_WIDTH, PAIR_WIDTH)),
    ]
    widths = [A_WIDTH, A_WIDTH, B_WIDTH, 2 * B_WIDTH, D_MODEL, D_MODEL]
    return pl.pallas_call(
        _sample_proj_kernel,
        grid=(1,),
        in_specs=in_specs,
        out_specs=[_full((rows, w)) for w in widths],
        out_shape=[jax.ShapeDtypeStruct((rows, w), F32) for w in widths],
        compiler_params=_params(),
        name=f"proj_sample_l{layer}",
    )(x, p["g_mix"], p["w_main"], p["w_gates"], p["g_v"], p["g_q"], p["g_k"],
      p["cos_s"], p["sin_s"], p["mean_mat"])


def _sample_attn_kernel(qt_ref, kvt_ref, c0_ref, c1_ref, c2_ref, sel1_ref, sel2_ref,
                        out_ref, o_scr, l_scr):
    kv_rows = 2 * HEADS_PER_PAIR * HEAD_DIM
    for g, (c_ref, sel_ref) in enumerate(((c0_ref, None), (c1_ref, sel1_ref),
                                          (c2_ref, sel2_ref))):
        window, dil = DILATED_PAIRS[g]
        for b in range(SAMPLE_GROUP):
            col = slice(b, b + 1)
            if sel_ref is not None:
                blk = c_ref[0, b].reshape(kv_rows, window).astype(BF16)
                taps = jnp.dot(blk, sel_ref[...], preferred_element_type=F32)
            for h in range(HEADS_PER_PAIR):
                if sel_ref is None:
                    kt, vt = c_ref[0, b, 0, h], c_ref[0, b, 1, h]
                else:
                    kt = taps[h * HEAD_DIM:(h + 1) * HEAD_DIM]
                    vt = taps[(HEADS_PER_PAIR + h) * HEAD_DIM:(HEADS_PER_PAIR + h + 1) * HEAD_DIM]
                feat = slice(g * PAIR_WIDTH + h * HEAD_DIM, g * PAIR_WIDTH + (h + 1) * HEAD_DIM)
                kfeat = slice(2 * g * PAIR_WIDTH + h * HEAD_DIM,
                              2 * g * PAIR_WIDTH + (h + 1) * HEAD_DIM)
                vfeat = slice((2 * g + 1) * PAIR_WIDTH + h * HEAD_DIM,
                              (2 * g + 1) * PAIR_WIDTH + (h + 1) * HEAD_DIM)
                q = qt_ref[0, feat, col]
                k_new = kvt_ref[0, kfeat, col]
                v_new = kvt_ref[0, vfeat, col]
                s = jnp.sum(kt * q, axis=0, keepdims=True)
                s_new = jnp.sum(k_new * q, axis=0, keepdims=True)
                mx = jnp.maximum(jnp.max(s, axis=1, keepdims=True), s_new)
                pr = jnp.exp(s - mx)
                p_new = jnp.exp(s_new - mx)
                den = jnp.sum(pr, axis=1, keepdims=True) + p_new
                o = (jnp.sum(vt * pr, axis=1, keepdims=True) + p_new * v_new) / den
                out_rows = slice(h * HEAD_DIM, (h + 1) * HEAD_DIM)
                o_scr[g, out_rows, col] = o
                l_scr[g, out_rows, col] = jnp.broadcast_to(mx + jnp.log(den), (HEAD_DIM, 1))
    l0, l1, l2 = l_scr[0], l_scr[1], l_scr[2]
    m = jnp.maximum(jnp.maximum(l0, l1), l2)
    e0, e1, e2 = jnp.exp(l0 - m), jnp.exp(l1 - m), jnp.exp(l2 - m)
    den = e0 + e1 + e2
    out_ref[0] = (e0 / den) * o_scr[0] + (e1 / den) * o_scr[1] + (e2 / den) * o_scr[2]


def _attn_sample(layer, qt, kvt, caches, tap_select):
    n = SAMPLE_GROUP
    groups = qt.shape[0]
    in_specs = [pl.BlockSpec((1, B_WIDTH, n), lambda i: (i, 0, 0)),
                pl.BlockSpec((1, 2 * B_WIDTH, n), lambda i: (i, 0, 0))]
    for (window, dil) in DILATED_PAIRS:
        in_specs.append(pl.BlockSpec((1, n, 2, HEADS_PER_PAIR, HEAD_DIM, window),
                                     lambda i: (layer, i, 0, 0, 0, 0)))
    in_specs += [_full(sel.shape) for sel in tap_select]
    scr = pltpu.VMEM((N_PAIRS, PAIR_WIDTH, n), F32)
    return pl.pallas_call(
        _sample_attn_kernel,
        grid=(groups,),
        in_specs=in_specs,
        out_specs=pl.BlockSpec((1, PAIR_WIDTH, n), lambda i: (i, 0, 0)),
        out_shape=jax.ShapeDtypeStruct((groups, PAIR_WIDTH, n), F32),
        scratch_shapes=[scr, scr],
        compiler_params=_params(),
        name=f"attn_sample_l{layer}",
    )(qt, kvt, *caches, *tap_select)


def _sample_mix_kernel(x_ref, u_ref, va_ref, bo_ref, ga_ref, gb_ref, wdiag_ref, bias0_ref,
                       wa_ref, wb_ref, wo_ref, out_ref):
    a_out = u_ref[...] * (wdiag_ref[...] * va_ref[...] + bias0_ref[...])
    out_ref[...] = _gated_merge(x_ref[...], a_out, bo_ref[...], ga_ref[...], gb_ref[...],
                                wa_ref, wb_ref, wo_ref)


def _mix_sample(layer, x, u, va, b_out, ga, gb, p):
    rows = x.shape[0]
    in_specs = [_full((rows, D_MODEL)), _full((rows, A_WIDTH)), _full((rows, A_WIDTH)),
                _full((rows, PAIR_WIDTH)), _full((rows, D_MODEL)), _full((rows, D_MODEL)),
                _layer_block((1, A_WIDTH), layer), _layer_block((1, A_WIDTH), layer),
                _layer_block((A_WIDTH, D_MODEL), layer),
                _layer_block((PAIR_WIDTH, D_MODEL), layer),
                _layer_block((D_MODEL, D_MODEL), layer)]
    return pl.pallas_call(
        _sample_mix_kernel,
        grid=(1,),
        in_specs=in_specs,
        out_specs=_full((rows, D_MODEL)),
        out_shape=jax.ShapeDtypeStruct((rows, D_MODEL), F32),
        compiler_params=_params(),
        name=f"mix_sample_l{layer}",
    )(x, u, va, b_out, ga, gb, p["wdiag"], p["bias0"], p["w_a"], p["w_b"], p["w_o"])


def _sample_ffn_kernel(x_ref, g_ref, wg_ref, wv_ref, cwg_ref, cwv_ref, cbg_ref, cbv_ref,
                       h0g_ref, h0v_ref, h1g_ref, h1v_ref, wdown_ref,
                       out_ref, upg_ref, upv_ref):
    j = pl.program_id(0)
    x = x_ref[...]
    xn = _rms(x, g_ref[...]).astype(BF16)

    def up_and_conv(w_ref, cw_ref, cb_ref, h0_ref, h1_ref, up_ref):
        up = jnp.dot(xn, w_ref[...], preferred_element_type=F32)
        up_ref[...] = up
        return (cb_ref[...] + cw_ref[0:1, :] * h0_ref[...] + cw_ref[1:2, :] * h1_ref[...]
                + cw_ref[2:3, :] * up)

    c_gate = up_and_conv(wg_ref, cwg_ref, cbg_ref, h0g_ref, h1g_ref, upg_ref)
    c_val = up_and_conv(wv_ref, cwv_ref, cbv_ref, h0v_ref, h1v_ref, upv_ref)
    h = (jax.nn.silu(c_gate) * c_val).astype(BF16)
    part = jnp.dot(h, wdown_ref[...], preferred_element_type=F32)

    @pl.when(j == 0)
    def _():
        out_ref[...] = x + part

    @pl.when(j > 0)
    def _():
        out_ref[...] += part


def _ffn_sample(layer, x, p, state_conv):
    rows = x.shape[0]
    nch = N_FF_CHUNKS
    fc = FF_CHUNK

    def cols(nrows, blk):
        return pl.BlockSpec((None, nrows, fc), lambda j: (layer, 0, blk * nch + j))

    state_flat = state_conv.reshape(DEPTH, rows, (CONV_W - 1) * 2 * D_FF)
    hist = [cols(rows, 2 * row + half) for row in range(CONV_W - 1) for half in range(2)]
    in_specs = [
        _full((rows, D_MODEL)),
        _layer_block((1, D_MODEL), layer),
        cols(D_MODEL, 0), cols(D_MODEL, 1),
        cols(CONV_W, 0), cols(CONV_W, 1),
        cols(1, 0), cols(1, 1),
        *hist,
        pl.BlockSpec((None, fc, D_MODEL), lambda j: (layer, j, 0)),
    ]
    up_spec = pl.BlockSpec((rows, fc), lambda j: (0, j))
    up_shape = jax.ShapeDtypeStruct((rows, D_FF), F32)
    return pl.pallas_call(
        _sample_ffn_kernel,
        grid=(nch,),
        in_specs=in_specs,
        out_specs=[_full((rows, D_MODEL)), up_spec, up_spec],
        out_shape=[jax.ShapeDtypeStruct((rows, D_MODEL), F32), up_shape, up_shape],
        compiler_params=_params(),
        name=f"ffn_sample_l{layer}",
    )(x, p["g_ffn"], p["w_up"], p["w_up"], p["conv_w"], p["conv_w"], p["conv_b"], p["conv_b"],
      state_flat, state_flat, state_flat, state_flat, p["w_down"])


def _sample_layer(layer, xs, p, caches_t, state_conv):
    n = SAMPLE_GROUP
    rows = xs.shape[0]
    u, va, q, kv, ga, gb = _proj_sample(layer, xs, p)

    def columns(a):
        return a.reshape(rows // n, n, a.shape[1]).swapaxes(1, 2)

    bt = _attn_sample(layer, columns(q), columns(kv), caches_t, p["tap_select"])
    b_out = bt.swapaxes(1, 2).reshape(rows, PAIR_WIDTH)
    xs = _mix_sample(layer, xs, u, va, b_out, ga, gb, p)
    xs, upg, upv = _ffn_sample(layer, xs, p, state_conv)
    up_row = jnp.concatenate([upg, upv], axis=-1)[:, None, :]
    new_conv = jnp.concatenate([state_conv[layer][:, 1:], up_row], axis=1)
    return xs, kv, va, new_conv


def _rope_tables(pos):
    half = HEAD_DIM // 2
    inv_freq = ROPE_THETA ** (-jnp.arange(half, dtype=F32) / half)
    ang = pos.astype(F32)[:, None] * inv_freq[None, :]
    cos, sin = jnp.cos(ang), jnp.sin(ang)
    cos_full = jnp.tile(jnp.concatenate([cos, cos], axis=1), (1, HEADS_PER_PAIR))
    sin_signed = jnp.tile(jnp.concatenate([-sin, sin], axis=1), (1, HEADS_PER_PAIR))
    return cos_full, sin_signed


def _prepare_params(g_mix, w_in, g_v, w_s, b_s, g_q, g_k, w_a_proj, w_b_proj, w_o,
                    g_ffn, w_up, conv_w, conv_b, w_down):
    head_of = jnp.arange(PAIR_WIDTH) // HEAD_DIM
    same_head = head_of[:, None] == head_of[None, :]
    bias_full = jnp.repeat(jnp.swapaxes(b_s, 1, 2), A_GROUP_DIM, axis=2)
    cos_p, sin_p = _rope_tables(jnp.arange(SEQ))
    cos_s, sin_s = _rope_tables(jnp.full((DEC_BATCH,), PAST_LEN))
    tap_select = [
        (jnp.arange(window)[:, None] == jnp.arange(TAPS)[None, :] * dil).astype(BF16)
        for (window, dil) in DILATED_PAIRS if dil > 1]
    return dict(
        tap_select=tap_select,
        g_mix=g_mix[:, None, :], g_v=g_v[:, None, :], g_ffn=g_ffn[:, None, :],
        g_q=jnp.tile(g_q, (1, HEADS_PER_PAIR))[:, None, :],
        g_k=jnp.tile(g_k, (1, HEADS_PER_PAIR))[:, None, :],
        w_main=w_in[:, :, :MAIN_COLS].astype(BF16),
        w_gates=w_in[:, :, OFF_GATES:].astype(BF16),
        w_a=w_a_proj.astype(BF16), w_b=w_b_proj.astype(BF16), w_o=w_o.astype(BF16),
        w_up=w_up.astype(BF16), w_down=w_down.astype(BF16),
        w_s=w_s, bias_full=bias_full, bias0=bias_full[:, 0:1, :],
        wdiag=jnp.repeat(w_s[:, :, 0, 0], A_GROUP_DIM, axis=1)[:, None, :],
        conv_w=conv_w, conv_b=conv_b[:, None, :],
        mean_mat=jnp.where(same_head, 1.0 / HEAD_DIM, 0.0).astype(BF16),
        cos_p=cos_p, sin_p=sin_p, cos_s=cos_s, sin_s=sin_s,
    )


def kernel(x_prompt, x_sample, cache_kv_w128, cache_kv_w512, cache_kv_w2048, state_conv,
           g_mix, w_in, g_v, w_s, b_s, g_q, g_k, w_a_proj, w_b_proj, w_o,
           g_ffn, w_up, conv_w, conv_b, w_down):
    assert x_prompt.shape == (BATCH, SEQ, D_MODEL) and x_sample.shape == (DEC_BATCH, 1, D_MODEL)
    p = _prepare_params(g_mix, w_in, g_v, w_s, b_s, g_q, g_k, w_a_proj, w_b_proj, w_o,
                        g_ffn, w_up, conv_w, conv_b, w_down)
    caches_t = []
    for cache, (window, dil) in zip((cache_kv_w128, cache_kv_w512, cache_kv_w2048), DILATED_PAIRS):
        assert cache.shape[2] == window
        caches_t.append(jnp.transpose(cache, (0, 1, 3, 4, 5, 2)))

    xp = x_prompt.reshape(BATCH * SEQ, D_MODEL)
    xs = x_sample.reshape(DEC_BATCH, D_MODEL)
    kv_prompt = None
    conv_p, conv_s, kv_s, va_s = [], [], [], []
    for l in range(DEPTH):
        outs = _proj_prompt(l, xp, p, kv_prompt)
        u, va = outs[0], outs[1]
        qkv = outs[2:11]
        kv_prompt = outs[11:14]
        b_out = _attn_prompt(l, qkv)
        xp = _mix_prompt(l, xp, u, va, b_out, p)
        xp, nconv = _ffn_prompt(l, xp, p)
        conv_p.append(nconv)

        xs, skv, sva, sconv = _sample_layer(l, xs, p, caches_t, state_conv)
        kv_s.append(skv)
        va_s.append(sva)
        conv_s.append(sconv)

    def kv_shape(a, rows):
        return a.reshape(DEPTH, -1, rows, 2, HEADS_PER_PAIR, HEAD_DIM)

    kv_s = jnp.stack(kv_s, axis=0)
    kv_s = [kv_s[:, :, 2 * g * PAIR_WIDTH:(2 * g + 2) * PAIR_WIDTH] for g in range(N_PAIRS)]
    return (
        xp.reshape(BATCH, SEQ, D_MODEL),
        xs.reshape(DEC_BATCH, 1, D_MODEL),
        kv_shape(kv_prompt[0], min(DILATED_PAIRS[0][0], SEQ)),
        kv_shape(kv_prompt[1], min(DILATED_PAIRS[1][0], SEQ)),
        kv_shape(kv_prompt[2], min(DILATED_PAIRS[2][0], SEQ)),
        jnp.stack(conv_p, axis=0),
        kv_shape(kv_s[0], 1),
        kv_shape(kv_s[1], 1),
        kv_shape(kv_s[2], 1),
        jnp.stack(conv_s, axis=0),
        jnp.stack(va_s, axis=0).reshape(DEPTH, DEC_BATCH, 1, A_WIDTH),
    )
```

```python
import functools

import jax
import jax.numpy as jnp
from jax import lax
from jax.experimental import pallas as pl
from jax.experimental.pallas import tpu as pltpu

D_MODEL = 1024
BATCH = 16
SEQ = 2048
DEPTH = 2
DEC_BATCH = 32
PAST_LEN = 16384
CHUNK = 128
A_GROUPS = 8
A_GROUP_DIM = 64
A_WIDTH = A_GROUPS * A_GROUP_DIM
HEAD_DIM = 64
HEADS_PER_PAIR = 4
DILATED_PAIRS = ((128, 1), (512, 4), (2048, 16))
N_PAIRS = len(DILATED_PAIRS)
TAPS = DILATED_PAIRS[0][0] // DILATED_PAIRS[0][1]
PAIR_WIDTH = HEADS_PER_PAIR * HEAD_DIM
B_WIDTH = N_PAIRS * PAIR_WIDTH
QBLK = 128
D_FF = 2816
CONV_W = 3
ROPE_THETA = 10000.0
EPS = 1e-6
NEG_INF = -1e30
LOG2E = 1.4426950408889634
LN2 = 0.6931471805599453
OFF_U, OFF_V = 0, A_WIDTH
OFF_Q = 2 * A_WIDTH
OFF_K = OFF_Q + B_WIDTH
OFF_VV = OFF_K + B_WIDTH
OFF_GATES = OFF_VV + B_WIDTH
MAIN_COLS = OFF_GATES

V7X_LANES = 128
V7X_VMEM_BYTES = 64 * 1024 * 1024
VMEM_LIMIT = V7X_VMEM_BYTES - 8 * 1024 * 1024

TOK_TILE = 512
TILES_PER_SEQ = SEQ // TOK_TILE
FFN_TILE = 1024
PROJ_TILE = 512
HALO = 16
FF_CHUNK = 256
N_FF_CHUNKS = D_FF // FF_CHUNK
ATTN_UNROLL = 16
SAMPLE_GROUP = 2
SAMPLE_FF_CHUNK = D_FF // 2

F32 = jnp.float32
BF16 = jnp.bfloat16

assert all(w // d == TAPS for (w, d) in DILATED_PAIRS) and TAPS >= QBLK - 1


def _params(**kw):
    return pltpu.CompilerParams(dimension_semantics=("arbitrary",),
                                vmem_limit_bytes=VMEM_LIMIT, **kw)


def _rms(x, g):
    return x * lax.rsqrt(jnp.mean(x * x, axis=-1, keepdims=True) + EPS) * g


def _swap_halves(t):
    lane = lax.broadcasted_iota(jnp.int32, (1, V7X_LANES), 1)
    first_half = (lane % HEAD_DIM) < (HEAD_DIM // 2)
    slabs = []
    for j in range(t.shape[1] // V7X_LANES):
        s = t[:, j * V7X_LANES:(j + 1) * V7X_LANES]
        up = pltpu.roll(s, V7X_LANES - HEAD_DIM // 2, 1)
        down = pltpu.roll(s, HEAD_DIM // 2, 1)
        slabs.append(jnp.where(first_half, up, down))
    return jnp.concatenate(slabs, axis=1)


def _head_norm_rope(t, gain, mean_mat, cos, sin_signed):
    ms = jnp.dot((t * t).astype(BF16), mean_mat, preferred_element_type=F32)
    tn = t * lax.rsqrt(ms + EPS) * gain
    return tn * cos + _swap_halves(tn) * sin_signed


def _full(shape):
    return pl.BlockSpec(shape, lambda i: (0,) * len(shape))


def _layer_block(shape, layer):
    return pl.BlockSpec((None,) + tuple(shape), lambda i: (layer,) + (0,) * len(shape))


def _resident_layer_block(shape, layer):
    return pl.BlockSpec((None,) + tuple(shape), lambda i: (layer,) + (0,) * len(shape),
                        pipeline_mode=pl.Buffered(1))


def _qkv_pair(proj, g, gq, gk, mean_mat, cos, sin_signed, q_scale=HEAD_DIM ** -0.5):
    c = g * PAIR_WIDTH
    q = _head_norm_rope(proj(OFF_Q + c, PAIR_WIDTH), gq, mean_mat, cos, sin_signed)
    q = q * q_scale
    k = _head_norm_rope(proj(OFF_K + c, PAIR_WIDTH), gk, mean_mat, cos, sin_signed)
    v = proj(OFF_VV + c, PAIR_WIDTH)
    return q, k, v


def _proj_kernel(x_ref, gmix_ref, w_ref, gv_ref, gq_ref, gk_ref, cos_ref, sin_ref, mean_ref,
                 *refs, n_prev):
    prev_kv_refs = refs[:N_PAIRS] if n_prev else (None,) * N_PAIRS
    (u_ref, va_ref, q0_ref, k0_ref, v0_ref, q1_ref, k1_ref, v1_ref, q2_ref, k2_ref, v2_ref,
     kv0_ref, kv1_ref, kv2_ref, scr_ref) = refs[N_PAIRS if n_prev else 0:]
    rows = PROJ_TILE
    qkv_refs = ((q0_ref, k0_ref, v0_ref), (q1_ref, k1_ref, v1_ref), (q2_ref, k2_ref, v2_ref))
    kv_refs = (kv0_ref, kv1_ref, kv2_ref)
    xn = _rms(x_ref[...], gmix_ref[...]).astype(BF16)

    def proj(c0, n):
        return jnp.dot(xn, w_ref[:, c0:c0 + n], preferred_element_type=F32)

    cos = cos_ref[...]
    sin_signed = sin_ref[...]
    mean_mat = mean_ref[...]
    n_slabs = PAIR_WIDTH // V7X_LANES

    def store_residue_major(dst_ref, slot, val, dil):
        for j in range(n_slabs):
            scr_ref[slot, j] = val[:, j * V7X_LANES:(j + 1) * V7X_LANES]
        for r in range(dil):
            picked = [scr_ref[slot, j, pl.ds(r, rows // dil, stride=dil), :]
                      for j in range(n_slabs)]
            dst_ref[0, r] = jnp.concatenate(picked, axis=1).astype(dst_ref.dtype)

    slots = iter(range(scr_ref.shape[0]))

    def emit(g, which, val):
        window, dil = DILATED_PAIRS[g]
        if dil == 1:
            qkv_refs[g][which][...] = val.astype(BF16)
        else:
            store_residue_major(qkv_refs[g][which], next(slots), val, dil)
        if which:
            keep = min(min(window, SEQ), rows)
            cols = slice((which - 1) * PAIR_WIDTH, which * PAIR_WIDTH)
            kv_refs[g][n_prev, 0, :, cols] = val[rows - keep:]

    def pair(g):
        if n_prev:
            kv_refs[g][:n_prev] = prev_kv_refs[g][...]
        q, k, v = _qkv_pair(proj, g, gq_ref[...], gk_ref[...], mean_mat, cos, sin_signed,
                            q_scale=HEAD_DIM ** -0.5 * LOG2E)
        for which, val in enumerate((q, k, v)):
            emit(g, which, val)

    pair(2)
    va_ref[...] = _rms(jax.nn.gelu(proj(OFF_V, A_WIDTH)), gv_ref[...]).astype(va_ref.dtype)
    pair(1)
    u_ref[...] = jax.nn.gelu(proj(OFF_U, A_WIDTH)).astype(u_ref.dtype)
    pair(0)


def _proj_prompt(layer, x, p, kv_prev):
    rows = PROJ_TILE
    n_tok = x.shape[0]
    nt = SEQ // rows

    def tok(width):
        return pl.BlockSpec((rows, width), lambda i: (i, 0))

    in_specs = [
        tok(D_MODEL),
        _layer_block((1, D_MODEL), layer),
        _resident_layer_block((D_MODEL, MAIN_COLS), layer),
        _layer_block((1, A_WIDTH), layer),
        _layer_block((1, PAIR_WIDTH), layer),
        _layer_block((1, PAIR_WIDTH), layer),
        pl.BlockSpec((rows, PAIR_WIDTH), lambda i: (i % nt, 0)),
        pl.BlockSpec((rows, PAIR_WIDTH), lambda i: (i % nt, 0)),
        _full((PAIR_WIDTH, PAIR_WIDTH)),
    ]
    out_shape = [jax.ShapeDtypeStruct((n_tok, A_WIDTH), BF16)] * 2
    out_specs = [tok(A_WIDTH)] * 2
    n_strided = 0
    for (window, dil) in DILATED_PAIRS:
        if dil == 1:
            out_shape += [jax.ShapeDtypeStruct((n_tok, PAIR_WIDTH), BF16)] * 3
            out_specs += [tok(PAIR_WIDTH)] * 3
        else:
            n_strided += 3
            out_shape += [jax.ShapeDtypeStruct((BATCH, dil, SEQ // dil, PAIR_WIDTH), BF16)] * 3
            out_specs += [pl.BlockSpec((1, dil, rows // dil, PAIR_WIDTH),
                                       lambda i: (i // nt, 0, i % nt, 0))] * 3
    args = [x, p["g_mix"], p["w_main"], p["g_v"], p["g_q"], p["g_k"], p["cos_p"], p["sin_p"],
            p["mean_mat"]]
    for g, (window, dil) in enumerate(DILATED_PAIRS):
        keep = min(window, SEQ)
        first_tile = (SEQ - max(keep, rows)) // rows

        def kv_spec(n_layers, ft=first_tile, blk=min(keep, rows)):
            return pl.BlockSpec((n_layers, 1, blk, 2 * PAIR_WIDTH),
                                lambda i: (0, i // nt, jnp.maximum(i % nt - ft, 0), 0))

        out_shape.append(jax.ShapeDtypeStruct((layer + 1, BATCH, keep, 2 * PAIR_WIDTH), F32))
        out_specs.append(kv_spec(layer + 1))
        if layer > 0:
            in_specs.append(kv_spec(layer))
            args.append(kv_prev[g])

    return pl.pallas_call(
        functools.partial(_proj_kernel, n_prev=layer),
        grid=(n_tok // rows,),
        in_specs=in_specs,
        out_specs=out_specs,
        out_shape=out_shape,
        scratch_shapes=[pltpu.VMEM((n_strided, PAIR_WIDTH // V7X_LANES, rows, V7X_LANES), F32)],
        compiler_params=_params(),
        name=f"proj_prompt_l{layer}",
    )(*args)


def _attn_block(q, kk, vv, valid):
    nk = kk.shape[0]
    lane_head = lax.broadcasted_iota(jnp.int32, (1, PAIR_WIDTH), 1) // HEAD_DIM
    head_masks = [lane_head == h for h in range(HEADS_PER_PAIR)]
    qs = jnp.concatenate([jnp.where(hm, q, jnp.zeros_like(q)) for hm in head_masks], axis=0)
    s = lax.dot_general(qs, kk, (((1,), (1,)), ((), ())), preferred_element_type=F32)
    s = jnp.where(valid[None], s.reshape(HEADS_PER_PAIR, QBLK, nk), NEG_INF)
    mx = jnp.max(s, axis=-1, keepdims=True)
    p = jnp.exp2(s - mx)
    den = jnp.sum(p, axis=-1, keepdims=True)
    pv = jnp.dot(p.reshape(HEADS_PER_PAIR * QBLK, nk).astype(BF16), vv,
                 preferred_element_type=F32).reshape(HEADS_PER_PAIR, QBLK, PAIR_WIDTH)
    inv_den = 1.0 / den
    lse_h = mx * LN2 + jnp.log(den)
    o, inv, lse = pv[0], inv_den[0], lse_h[0]
    for h in range(1, HEADS_PER_PAIR):
        o = jnp.where(head_masks[h], pv[h], o)
        inv = jnp.where(head_masks[h], inv_den[h], inv)
        lse = jnp.where(head_masks[h], lse_h[h], lse)
    return o * inv, jnp.broadcast_to(lse, (QBLK, PAIR_WIDTH))


def _attn_kernel(q0_ref, k0_ref, v0_ref, q1_ref, k1_ref, v1_ref, q2_ref, k2_ref, v2_ref,
                 out_ref, o_scr, l_scr):
    qi = lax.broadcasted_iota(jnp.int32, (QBLK, QBLK), 0)
    ki = lax.broadcasted_iota(jnp.int32, (QBLK, QBLK), 1)
    causal = ki <= qi
    qi2 = lax.broadcasted_iota(jnp.int32, (QBLK, 2 * QBLK), 0)
    ki2 = lax.broadcasted_iota(jnp.int32, (QBLK, 2 * QBLK), 1)
    dist = qi2 - ki2 + QBLK
    band = (dist >= 0) & (dist <= TAPS)
    n_slabs = PAIR_WIDTH // V7X_LANES

    def put(scr, g, rows, val):
        for j in range(n_slabs):
            scr[g, j, rows, :] = val[:, j * V7X_LANES:(j + 1) * V7X_LANES]

    def get(scr, g, rows):
        return jnp.concatenate([scr[g, j, rows, :] for j in range(n_slabs)], axis=1)

    for g, refs in enumerate(((q0_ref, k0_ref, v0_ref), (q1_ref, k1_ref, v1_ref),
                              (q2_ref, k2_ref, v2_ref))):
        dil = DILATED_PAIRS[g][1]
        n_blocks = (SEQ // dil) // QBLK

        def rows_of(ref, r, start, dil=dil):
            if dil == 1:
                return ref[pl.ds(start, QBLK), :]
            return ref[0, r, pl.ds(start, QBLK), :]

        def step(idx, carry, g=g, dil=dil, n_blocks=n_blocks, refs=refs, rows_of=rows_of):
            q_ref, k_ref, v_ref = refs
            r, i = idx // n_blocks, idx % n_blocks
            cur = pl.multiple_of(i * QBLK, QBLK)
            q = rows_of(q_ref, r, cur)
            if n_blocks == 1:
                o, lse = _attn_block(q, rows_of(k_ref, r, cur), rows_of(v_ref, r, cur), causal)
            else:
                prev = pl.multiple_of(jnp.maximum(i - 1, 0) * QBLK, QBLK)
                kk = jnp.concatenate([rows_of(k_ref, r, prev), rows_of(k_ref, r, cur)], axis=0)
                vv = jnp.concatenate([rows_of(v_ref, r, prev), rows_of(v_ref, r, cur)], axis=0)
                first_key = jnp.where(i > 0, 0, QBLK)
                o, lse = _attn_block(q, kk, vv, band & (ki2 >= first_key))
            if dil == 1:
                rows = pl.ds(cur, QBLK)
            else:
                rows = pl.ds(r + i * (QBLK * dil), QBLK, stride=dil)
            put(o_scr, g, rows, o)
            put(l_scr, g, rows, lse)
            return carry

        lax.fori_loop(0, dil * n_blocks, step, 0, unroll=ATTN_UNROLL)

    step_rows = 256
    for t0 in range(0, SEQ, step_rows):
        rows = slice(t0, t0 + step_rows)
        l0, l1, l2 = get(l_scr, 0, rows), get(l_scr, 1, rows), get(l_scr, 2, rows)
        m = jnp.maximum(jnp.maximum(l0, l1), l2)
        e0, e1, e2 = jnp.exp(l0 - m), jnp.exp(l1 - m), jnp.exp(l2 - m)
        den = e0 + e1 + e2
        merged = ((e0 / den) * get(o_scr, 0, rows) + (e1 / den) * get(o_scr, 1, rows)
                  + (e2 / den) * get(o_scr, 2, rows))
        out_ref[rows, :] = merged.astype(out_ref.dtype)


def _attn_prompt(layer, qkv):
    n_tok = qkv[0].shape[0]
    in_specs = []
    for (window, dil) in DILATED_PAIRS:
        if dil == 1:
            in_specs += [pl.BlockSpec((SEQ, PAIR_WIDTH), lambda b: (b, 0))] * 3
        else:
            in_specs += [pl.BlockSpec((1, dil, SEQ // dil, PAIR_WIDTH),
                                      lambda b: (b, 0, 0, 0))] * 3
    scr = pltpu.VMEM((N_PAIRS, PAIR_WIDTH // V7X_LANES, SEQ, V7X_LANES), F32)
    return pl.pallas_call(
        _attn_kernel,
        grid=(BATCH,),
        in_specs=in_specs,
        out_specs=pl.BlockSpec((SEQ, PAIR_WIDTH), lambda b: (b, 0)),
        out_shape=jax.ShapeDtypeStruct((n_tok, PAIR_WIDTH), BF16),
        scratch_shapes=[scr, scr],
        compiler_params=_params(),
        name=f"attn_prompt_l{layer}",
    )(*qkv)


def _gated_merge(x, a_out, b_out, ga, gb, wa_ref, wb_ref, wo_ref):
    ap = jnp.dot(a_out.astype(BF16), wa_ref[...], preferred_element_type=F32)
    bp = jnp.dot(b_out.astype(BF16), wb_ref[...], preferred_element_type=F32)
    merged = ga * ap + gb * bp
    return x + jnp.dot(merged.astype(BF16), wo_ref[...], preferred_element_type=F32)


def _gates(xn, wg_ref):
    ga = jax.nn.sigmoid(jnp.dot(xn, wg_ref[:, :D_MODEL], preferred_element_type=F32))
    gb = jax.nn.sigmoid(jnp.dot(xn, wg_ref[:, D_MODEL:], preferred_element_type=F32))
    return ga, gb


def _mix_kernel(x_ref, u_ref, va_ref, bo_ref, gmix_ref, wg_ref, ws_ref, bias_ref,
                wa_ref, wb_ref, wo_ref, out_ref, a_scr):
    ti = lax.broadcasted_iota(jnp.int32, (CHUNK, CHUNK), 0)
    si = lax.broadcasted_iota(jnp.int32, (CHUNK, CHUNK), 1)
    tril = si <= ti
    w_tril = [jnp.where(tril, ws_ref[g], 0.0).astype(BF16) for g in range(A_GROUPS)]
    low_group = lax.broadcasted_iota(jnp.int32, (1, V7X_LANES), 1) < A_GROUP_DIM
    bias = bias_ref[...]
    groups_per_slab = V7X_LANES // A_GROUP_DIM
    for c in range(TOK_TILE // CHUNK):
        rows = slice(c * CHUNK, (c + 1) * CHUNK)
        zs = []
        for j in range(A_WIDTH // V7X_LANES):
            slab = va_ref[rows, j * V7X_LANES:(j + 1) * V7X_LANES]
            z_lo = jnp.dot(w_tril[groups_per_slab * j], slab, preferred_element_type=F32)
            z_hi = jnp.dot(w_tril[groups_per_slab * j + 1], slab, preferred_element_type=F32)
            zs.append(jnp.where(low_group, z_lo, z_hi))
        z = jnp.concatenate(zs, axis=1) + bias
        a_scr[rows, :] = (u_ref[rows, :].astype(F32) * z).astype(BF16)
    x = x_ref[...]
    ga, gb = _gates(_rms(x, gmix_ref[...]).astype(BF16), wg_ref)
    out_ref[...] = _gated_merge(x, a_scr[...], bo_ref[...], ga, gb, wa_ref, wb_ref, wo_ref)


def _mix_prompt(layer, x, u, va, b_out, p):
    rows = TOK_TILE
    n_tok = x.shape[0]

    def tok(width):
        return pl.BlockSpec((rows, width), lambda i: (i, 0))

    in_specs = [
        tok(D_MODEL), tok(A_WIDTH), tok(A_WIDTH), tok(PAIR_WIDTH),
        _layer_block((1, D_MODEL), layer),
        _layer_block((D_MODEL, 2 * D_MODEL), layer),
        _layer_block((A_GROUPS, CHUNK, CHUNK), layer),
        _layer_block((CHUNK, A_WIDTH), layer),
        _layer_block((A_WIDTH, D_MODEL), layer),
        _layer_block((PAIR_WIDTH, D_MODEL), layer),
        _layer_block((D_MODEL, D_MODEL), layer),
    ]
    return pl.pallas_call(
        _mix_kernel,
        grid=(n_tok // rows,),
        in_specs=in_specs,
        out_specs=tok(D_MODEL),
        out_shape=jax.ShapeDtypeStruct((n_tok, D_MODEL), F32),
        scratch_shapes=[pltpu.VMEM((rows, A_WIDTH), BF16)],
        compiler_params=_params(),
        name=f"mix_prompt_l{layer}",
    )(x, u, va, b_out, p["g_mix"], p["w_gates"], p["w_s"], p["bias_full"],
      p["w_a"], p["w_b"], p["w_o"])


def _ffn_kernel(x_ref, halo_ref, g_ref, wup_ref, cw_ref, cb_ref, wdown_ref,
                out_ref, conv_ref, xn_scr, up_scr, h_scr):
    rows = FFN_TILE
    tile = pl.program_id(0) % (SEQ // FFN_TILE)
    x = x_ref[...]
    g = g_ref[...]
    halo_on = jnp.where(tile == 0, 0.0, 1.0)
    xn_scr[:HALO, :] = (_rms(halo_ref[...], g) * halo_on).astype(BF16)
    xn_scr[HALO:, :] = _rms(x, g).astype(BF16)
    xn = xn_scr[...]
    n_slabs = FF_CHUNK // V7X_LANES

    def up_and_conv(c0, slot):
        cols = slice(c0, c0 + FF_CHUNK)
        up = jnp.dot(xn, wup_ref[:, cols], preferred_element_type=F32)
        conv_ref[0, :, cols] = up[HALO + rows - (CONV_W - 1):]
        for s in range(n_slabs):
            up_scr[slot, s] = up[:, s * V7X_LANES:(s + 1) * V7X_LANES]
        parts = []
        for s in range(n_slabs):
            cs = slice(c0 + s * V7X_LANES, c0 + (s + 1) * V7X_LANES)
            acc = cb_ref[:, cs]
            for j in range(CONV_W - 1):
                lo = HALO - (CONV_W - 1) + j
                acc = acc + cw_ref[j:j + 1, cs] * up_scr[slot, s, lo:lo + rows, :]
            parts.append(acc + cw_ref[CONV_W - 1:CONV_W, cs]
                         * up[HALO:, s * V7X_LANES:(s + 1) * V7X_LANES])
        return jnp.concatenate(parts, axis=1)

    for j in range(N_FF_CHUNKS):
        c_gate = up_and_conv(j * FF_CHUNK, 0)
        c_val = up_and_conv(D_FF + j * FF_CHUNK, 1)
        h_scr[:, j * FF_CHUNK:(j + 1) * FF_CHUNK] = (jax.nn.silu(c_gate) * c_val).astype(BF16)
    out_ref[...] = x + jnp.dot(h_scr[...], wdown_ref[...], preferred_element_type=F32)


def _ffn_prompt(layer, x, p):
    rows = FFN_TILE
    n_tok = x.shape[0]
    nt = SEQ // rows
    halo_per_tile = rows // HALO
    in_specs = [
        pl.BlockSpec((rows, D_MODEL), lambda i: (i, 0)),
        pl.BlockSpec((HALO, D_MODEL), lambda i: (jnp.maximum(i * halo_per_tile - 1, 0), 0)),
        _layer_block((1, D_MODEL), layer),
        _resident_layer_block((D_MODEL, 2 * D_FF), layer),
        _layer_block((CONV_W, 2 * D_FF), layer),
        _layer_block((1, 2 * D_FF), layer),
        _resident_layer_block((D_FF, D_MODEL), layer),
    ]
    return pl.pallas_call(
        _ffn_kernel,
        grid=(n_tok // rows,),
        in_specs=in_specs,
        out_specs=[pl.BlockSpec((rows, D_MODEL), lambda i: (i, 0)),
                   pl.BlockSpec((1, CONV_W - 1, 2 * D_FF), lambda i: (i // nt, 0, 0))],
        out_shape=[jax.ShapeDtypeStruct((n_tok, D_MODEL), F32),
                   jax.ShapeDtypeStruct((BATCH, CONV_W - 1, 2 * D_FF), F32)],
        scratch_shapes=[pltpu.VMEM((HALO + rows, D_MODEL), BF16),
                        pltpu.VMEM((2, FF_CHUNK // V7X_LANES, HALO + rows, V7X_LANES), F32),
                        pltpu.VMEM((rows, D_FF), BF16)],
        compiler_params=_params(),
        name=f"ffn_prompt_l{layer}",
    )(x, x, p["g_ffn"], p["w_up"], p["conv_w"], p["conv_b"], p["w_down"])


def _sample_proj_kernel(x_ref, gmix_ref, w_ref, wg_ref, gv_ref, gq_ref, gk_ref, cos_ref, sin_ref,
                        mean_ref, u_ref, va_ref, q_ref, kv_ref, ga_ref, gb_ref):
    xn = _rms(x_ref[...], gmix_ref[...]).astype(BF16)

    def proj(c0, n):
        return jnp.dot(xn, w_ref[:, c0:c0 + n], preferred_element_type=F32)

    u_ref[...] = jax.nn.gelu(proj(OFF_U, A_WIDTH))
    va_ref[...] = _rms(jax.nn.gelu(proj(OFF_V, A_WIDTH)), gv_ref[...])
    ga_ref[...], gb_ref[...] = _gates(xn, wg_ref)
    for g in range(N_PAIRS):
        q, k, v = _qkv_pair(proj, g, gq_ref[...], gk_ref[...], mean_ref[...], cos_ref[...],
                            sin_ref[...])
        q_ref[:, g * PAIR_WIDTH:(g + 1) * PAIR_WIDTH] = q
        kv_ref[:, 2 * g * PAIR_WIDTH:(2 * g + 1) * PAIR_WIDTH] = k
        kv_ref[:, (2 * g + 1) * PAIR_WIDTH:(2 * g + 2) * PAIR_WIDTH] = v


def _proj_sample(layer, x, p):
    rows = x.shape[0]
    in_specs = [
        _full((rows, D_MODEL)),
        _layer_block((1, D_MODEL), layer),
        _layer_block((D_MODEL, MAIN_COLS), layer),
        _layer_block((D_MODEL, 2 * D_MODEL), layer),
        _layer_block((1, A_WIDTH), layer),
        _layer_block((1, PAIR_WIDTH), layer),
        _layer_block((1, PAIR_WIDTH), layer),
        _full((rows, PAIR_WIDTH)),
        _full((rows, PAIR_WIDTH)),
        _full((PAIR_WIDTH, PAIR_WIDTH)),
    ]
    widths = [A_WIDTH, A_WIDTH, B_WIDTH, 2 * B_WIDTH, D_MODEL, D_MODEL]
    return pl.pallas_call(
        _sample_proj_kernel,
        grid=(1,),
        in_specs=in_specs,
        out_specs=[_full((rows, w)) for w in widths],
        out_shape=[jax.ShapeDtypeStruct((rows, w), F32) for w in widths],
        compiler_params=_params(),
        name=f"proj_sample_l{layer}",
    )(x, p["g_mix"], p["w_main"], p["w_gates"], p["g_v"], p["g_q"], p["g_k"],
      p["cos_s"], p["sin_s"], p["mean_mat"])


def _sample_attn_kernel(qt_ref, kvt_ref, c0_ref, c1_ref, c2_ref, sel1_ref, sel2_ref,
                        out_ref, o_scr, l_scr):
    kv_rows = 2 * HEADS_PER_PAIR * HEAD_DIM
    for g, (c_ref, sel_ref) in enumerate(((c0_ref, None), (c1_ref, sel1_ref),
                                          (c2_ref, sel2_ref))):
        window, dil = DILATED_PAIRS[g]
        for b in range(SAMPLE_GROUP):
            col = slice(b, b + 1)
            if sel_ref is not None:
                blk = c_ref[0, b].reshape(kv_rows, window).astype(BF16)
                taps = jnp.dot(blk, sel_ref[...], preferred_element_type=F32)
            for h in range(HEADS_PER_PAIR):
                if sel_ref is None:
                    kt, vt = c_ref[0, b, 0, h], c_ref[0, b, 1, h]
                else:
                    kt = taps[h * HEAD_DIM:(h + 1) * HEAD_DIM]
                    vt = taps[(HEADS_PER_PAIR + h) * HEAD_DIM:(HEADS_PER_PAIR + h + 1) * HEAD_DIM]
                feat = slice(g * PAIR_WIDTH + h * HEAD_DIM, g * PAIR_WIDTH + (h + 1) * HEAD_DIM)
                kfeat = slice(2 * g * PAIR_WIDTH + h * HEAD_DIM,
                              2 * g * PAIR_WIDTH + (h + 1) * HEAD_DIM)
                vfeat = slice((2 * g + 1) * PAIR_WIDTH + h * HEAD_DIM,
                              (2 * g + 1) * PAIR_WIDTH + (h + 1) * HEAD_DIM)
                q = qt_ref[0, feat, col]
                k_new = kvt_ref[0, kfeat, col]
                v_new = kvt_ref[0, vfeat, col]
                s = jnp.sum(kt * q, axis=0, keepdims=True)
                s_new = jnp.sum(k_new * q, axis=0, keepdims=True)
                mx = jnp.maximum(jnp.max(s, axis=1, keepdims=True), s_new)
                pr = jnp.exp(s - mx)
                p_new = jnp.exp(s_new - mx)
                den = jnp.sum(pr, axis=1, keepdims=True) + p_new
                o = (jnp.sum(vt * pr, axis=1, keepdims=True) + p_new * v_new) / den
                out_rows = slice(h * HEAD_DIM, (h + 1) * HEAD_DIM)
                o_scr[g, out_rows, col] = o
                l_scr[g, out_rows, col] = jnp.broadcast_to(mx + jnp.log(den), (HEAD_DIM, 1))
    l0, l1, l2 = l_scr[0], l_scr[1], l_scr[2]
    m = jnp.maximum(jnp.maximum(l0, l1), l2)
    e0, e1, e2 = jnp.exp(l0 - m), jnp.exp(l1 - m), jnp.exp(l2 - m)
    den = e0 + e1 + e2
    out_ref[0] = (e0 / den) * o_scr[0] + (e1 / den) * o_scr[1] + (e2 / den) * o_scr[2]


def _attn_sample(layer, qt, kvt, caches, tap_select):
    n = SAMPLE_GROUP
    groups = qt.shape[0]
    in_specs = [pl.BlockSpec((1, B_WIDTH, n), lambda i: (i, 0, 0)),
                pl.BlockSpec((1, 2 * B_WIDTH, n), lambda i: (i, 0, 0))]
    for (window, dil) in DILATED_PAIRS:
        in_specs.append(pl.BlockSpec((1, n, 2, HEADS_PER_PAIR, HEAD_DIM, window),
                                     lambda i: (layer, i, 0, 0, 0, 0)))
    in_specs += [_full(sel.shape) for sel in tap_select]
    scr = pltpu.VMEM((N_PAIRS, PAIR_WIDTH, n), F32)
    return pl.pallas_call(
        _sample_attn_kernel,
        grid=(groups,),
        in_specs=in_specs,
        out_specs=pl.BlockSpec((1, PAIR_WIDTH, n), lambda i: (i, 0, 0)),
        out_shape=jax.ShapeDtypeStruct((groups, PAIR_WIDTH, n), F32),
        scratch_shapes=[scr, scr],
        compiler_params=_params(),
        name=f"attn_sample_l{layer}",
    )(qt, kvt, *caches, *tap_select)


def _sample_mix_kernel(x_ref, u_ref, va_ref, bo_ref, ga_ref, gb_ref, wdiag_ref, bias0_ref,
                       wa_ref, wb_ref, wo_ref, out_ref):
    a_out = u_ref[...] * (wdiag_ref[...] * va_ref[...] + bias0_ref[...])
    out_ref[...] = _gated_merge(x_ref[...], a_out, bo_ref[...], ga_ref[...], gb_ref[...],
                                wa_ref, wb_ref, wo_ref)


def _mix_sample(layer, x, u, va, b_out, ga, gb, p):
    rows = x.shape[0]
    in_specs = [_full((rows, D_MODEL)), _full((rows, A_WIDTH)), _full((rows, A_WIDTH)),
                _full((rows, PAIR_WIDTH)), _full((rows, D_MODEL)), _full((rows, D_MODEL)),
                _layer_block((1, A_WIDTH), layer), _layer_block((1, A_WIDTH), layer),
                _layer_block((A_WIDTH, D_MODEL), layer),
                _layer_block((PAIR_WIDTH, D_MODEL), layer),
                _layer_block((D_MODEL, D_MODEL), layer)]
    return pl.pallas_call(
        _sample_mix_kernel,
        grid=(1,),
        in_specs=in_specs,
        out_specs=_full((rows, D_MODEL)),
        out_shape=jax.ShapeDtypeStruct((rows, D_MODEL), F32),
        compiler_params=_params(),
        name=f"mix_sample_l{layer}",
    )(x, u, va, b_out, ga, gb, p["wdiag"], p["bias0"], p["w_a"], p["w_b"], p["w_o"])


def _sample_ffn_kernel(x_ref, g_ref, wg_ref, wv_ref, cwg_ref, cwv_ref, cbg_ref, cbv_ref,
                       h0g_ref, h0v_ref, h1g_ref, h1v_ref, wdown_ref,
                       out_ref, upg_ref, upv_ref):
    j = pl.program_id(0)
    x = x_ref[...]
    xn = _rms(x, g_ref[...]).astype(BF16)

    def up_and_conv(w_ref, cw_ref, cb_ref, h0_ref, h1_ref, up_ref):
        up = jnp.dot(xn, w_ref[...], preferred_element_type=F32)
        up_ref[...] = up
        return (cb_ref[...] + cw_ref[0:1, :] * h0_ref[...] + cw_ref[1:2, :] * h1_ref[...]
                + cw_ref[2:3, :] * up)

    c_gate = up_and_conv(wg_ref, cwg_ref, cbg_ref, h0g_ref, h1g_ref, upg_ref)
    c_val = up_and_conv(wv_ref, cwv_ref, cbv_ref, h0v_ref, h1v_ref, upv_ref)
    h = (jax.nn.silu(c_gate) * c_val).astype(BF16)
    part = jnp.dot(h, wdown_ref[...], preferred_element_type=F32)

    @pl.when(j == 0)
    def _():
        out_ref[...] = x + part

    @pl.when(j > 0)
    def _():
        out_ref[...] += part


def _ffn_sample(layer, x, p, state_conv):
    rows = x.shape[0]
    fc = SAMPLE_FF_CHUNK
    nch = D_FF // fc

    def cols(nrows, blk):
        return pl.BlockSpec((None, nrows, fc), lambda j: (layer, 0, blk * nch + j))

    state_flat = state_conv.reshape(DEPTH, rows, (CONV_W - 1) * 2 * D_FF)
    hist = [cols(rows, 2 * row + half) for row in range(CONV_W - 1) for half in range(2)]
    in_specs = [
        _full((rows, D_MODEL)),
        _layer_block((1, D_MODEL), layer),
        cols(D_MODEL, 0), cols(D_MODEL, 1),
        cols(CONV_W, 0), cols(CONV_W, 1),
        cols(1, 0), cols(1, 1),
        *hist,
        pl.BlockSpec((None, fc, D_MODEL), lambda j: (layer, j, 0)),
    ]
    up_spec = pl.BlockSpec((rows, fc), lambda j: (0, j))
    up_shape = jax.ShapeDtypeStruct((rows, D_FF), F32)
    return pl.pallas_call(
        _sample_ffn_kernel,
        grid=(nch,),
        in_specs=in_specs,
        out_specs=[_full((rows, D_MODEL)), up_spec, up_spec],
        out_shape=[jax.ShapeDtypeStruct((rows, D_MODEL), F32), up_shape, up_shape],
        compiler_params=_params(),
        name=f"ffn_sample_l{layer}",
    )(x, p["g_ffn"], p["w_up"], p["w_up"], p["conv_w"], p["conv_w"], p["conv_b"], p["conv_b"],
      state_flat, state_flat, state_flat, state_flat, p["w_down"])


def _sample_layer(layer, xs, p, caches_t, state_conv):
    n = SAMPLE_GROUP
    rows = xs.shape[0]
    u, va, q, kv, ga, gb = _proj_sample(layer, xs, p)

    def columns(a):
        return a.reshape(rows // n, n, a.shape[1]).swapaxes(1, 2)

    bt = _attn_sample(layer, columns(q), columns(kv), caches_t, p["tap_select"])
    b_out = bt.swapaxes(1, 2).reshape(rows, PAIR_WIDTH)
    xs = _mix_sample(layer, xs, u, va, b_out, ga, gb, p)
    xs, upg, upv = _ffn_sample(layer, xs, p, state_conv)
    up_row = jnp.concatenate([upg, upv], axis=-1)[:, None, :]
    new_conv = jnp.concatenate([state_conv[layer][:, 1:], up_row], axis=1)
    return xs, kv, va, new_conv


def _rope_tables(pos):
    half = HEAD_DIM // 2
    inv_freq = ROPE_THETA ** (-jnp.arange(half, dtype=F32) / half)
    ang = pos.astype(F32)[:, None] * inv_freq[None, :]
    cos, sin = jnp.cos(ang), jnp.sin(ang)
    cos_full = jnp.tile(jnp.concatenate([cos, cos], axis=1), (1, HEADS_PER_PAIR))
    sin_signed = jnp.tile(jnp.concatenate([-sin, sin], axis=1), (1, HEADS_PER_PAIR))
    return cos_full, sin_signed


def _prepare_params(g_mix, w_in, g_v, w_s, b_s, g_q, g_k, w_a_proj, w_b_proj, w_o,
                    g_ffn, w_up, conv_w, conv_b, w_down):
    head_of = jnp.arange(PAIR_WIDTH) // HEAD_DIM
    same_head = head_of[:, None] == head_of[None, :]
    bias_full = jnp.repeat(jnp.swapaxes(b_s, 1, 2), A_GROUP_DIM, axis=2)
    cos_p, sin_p = _rope_tables(jnp.arange(SEQ))
    cos_s, sin_s = _rope_tables(jnp.full((DEC_BATCH,), PAST_LEN))
    tap_select = [
        (jnp.arange(window)[:, None] == jnp.arange(TAPS)[None, :] * dil).astype(BF16)
        for (window, dil) in DILATED_PAIRS if dil > 1]
    w_in_b = w_in.astype(BF16)
    return dict(
        tap_select=tap_select,
        g_mix=g_mix[:, None, :], g_v=g_v[:, None, :], g_ffn=g_ffn[:, None, :],
        g_q=jnp.tile(g_q, (1, HEADS_PER_PAIR))[:, None, :],
        g_k=jnp.tile(g_k, (1, HEADS_PER_PAIR))[:, None, :],
        w_main=w_in_b[:, :, :MAIN_COLS], w_gates=w_in_b[:, :, OFF_GATES:],
        w_a=w_a_proj.astype(BF16), w_b=w_b_proj.astype(BF16), w_o=w_o.astype(BF16),
        w_up=w_up.astype(BF16), w_down=w_down.astype(BF16),
        w_s=w_s, bias_full=bias_full, bias0=bias_full[:, 0:1, :],
        wdiag=jnp.repeat(w_s[:, :, 0, 0], A_GROUP_DIM, axis=1)[:, None, :],
        conv_w=conv_w, conv_b=conv_b[:, None, :],
        mean_mat=jnp.where(same_head, 1.0 / HEAD_DIM, 0.0).astype(BF16),
        cos_p=cos_p, sin_p=sin_p, cos_s=cos_s, sin_s=sin_s,
    )


def kernel(x_prompt, x_sample, cache_kv_w128, cache_kv_w512, cache_kv_w2048, state_conv,
           g_mix, w_in, g_v, w_s, b_s, g_q, g_k, w_a_proj, w_b_proj, w_o,
           g_ffn, w_up, conv_w, conv_b, w_down):
    assert x_prompt.shape == (BATCH, SEQ, D_MODEL) and x_sample.shape == (DEC_BATCH, 1, D_MODEL)
    p = _prepare_params(g_mix, w_in, g_v, w_s, b_s, g_q, g_k, w_a_proj, w_b_proj, w_o,
                        g_ffn, w_up, conv_w, conv_b, w_down)
    caches_t = []
    for cache, (window, dil) in zip((cache_kv_w128, cache_kv_w512, cache_kv_w2048), DILATED_PAIRS):
        assert cache.shape[2] == window
        caches_t.append(jnp.transpose(cache, (0, 1, 3, 4, 5, 2)))

    xp = x_prompt.reshape(BATCH * SEQ, D_MODEL)
    xs = x_sample.reshape(DEC_BATCH, D_MODEL)
    kv_prompt = None
    conv_p, conv_s, kv_s, va_s = [], [], [], []
    for l in range(DEPTH):
        outs = _proj_prompt(l, xp, p, kv_prompt)
        u, va = outs[0], outs[1]
        qkv = outs[2:11]
        kv_prompt = outs[11:14]
        b_out = _attn_prompt(l, qkv)
        xp = _mix_prompt(l, xp, u, va, b_out, p)
        xp, nconv = _ffn_prompt(l, xp, p)
        conv_p.append(nconv)

        xs, skv, sva, sconv = _sample_layer(l, xs, p, caches_t, state_conv)
        kv_s.append(skv)
        va_s.append(sva)
        conv_s.append(sconv)

    def kv_shape(a, rows):
        return a.reshape(DEPTH, -1, rows, 2, HEADS_PER_PAIR, HEAD_DIM)

    kv_s = jnp.stack(kv_s, axis=0)
    kv_s = [kv_s[:, :, 2 * g * PAIR_WIDTH:(2 * g + 2) * PAIR_WIDTH] for g in range(N_PAIRS)]
    return (
        xp.reshape(BATCH, SEQ, D_MODEL),
        xs.reshape(DEC_BATCH, 1, D_MODEL),
        kv_shape(kv_prompt[0], min(DILATED_PAIRS[0][0], SEQ)),
        kv_shape(kv_prompt[1], min(DILATED_PAIRS[1][0], SEQ)),
        kv_shape(kv_prompt[2], min(DILATED_PAIRS[2][0], SEQ)),
        jnp.stack(conv_p, axis=0),
        kv_shape(kv_s[0], 1),
        kv_shape(kv_s[1], 1),
        kv_shape(kv_s[2], 1),
        jnp.stack(conv_s, axis=0),
        jnp.stack(va_s, axis=0).reshape(DEPTH, DEC_BATCH, 1, A_WIDTH),
    )
```

```python
import functools

import jax
import jax.numpy as jnp
from jax import lax
from jax.experimental import pallas as pl
from jax.experimental.pallas import tpu as pltpu

D_MODEL = 1024
BATCH = 16
SEQ = 2048
DEPTH = 2
DEC_BATCH = 32
PAST_LEN = 16384
CHUNK = 128
A_GROUPS = 8
A_GROUP_DIM = 64
A_WIDTH = A_GROUPS * A_GROUP_DIM
HEAD_DIM = 64
HEADS_PER_PAIR = 4
DILATED_PAIRS = ((128, 1), (512, 4), (2048, 16))
N_PAIRS = len(DILATED_PAIRS)
TAPS = DILATED_PAIRS[0][0] // DILATED_PAIRS[0][1]
PAIR_WIDTH = HEADS_PER_PAIR * HEAD_DIM
B_WIDTH = N_PAIRS * PAIR_WIDTH
QBLK = 128
D_FF = 2816
CONV_W = 3
ROPE_THETA = 10000.0
EPS = 1e-6
NEG_INF = -1e30
LOG2E = 1.4426950408889634
OFF_U, OFF_V = 0, A_WIDTH
OFF_Q = 2 * A_WIDTH
OFF_K = OFF_Q + B_WIDTH
OFF_VV = OFF_K + B_WIDTH
OFF_GATES = OFF_VV + B_WIDTH
MAIN_COLS = OFF_GATES

V7X_LANES = 128
V7X_VMEM_BYTES = 64 * 1024 * 1024
VMEM_LIMIT = V7X_VMEM_BYTES - 8 * 1024 * 1024

TOK_TILE = 512
TILES_PER_SEQ = SEQ // TOK_TILE
FFN_TILE = 1024
PROJ_TILE = 512
HALO = 16
FF_CHUNK = 256
N_FF_CHUNKS = D_FF // FF_CHUNK
ATTN_UNROLL = 16
SAMPLE_GROUP = 2
SAMPLE_FF_CHUNK = D_FF // 2

F32 = jnp.float32
BF16 = jnp.bfloat16

assert all(w // d == TAPS for (w, d) in DILATED_PAIRS) and TAPS >= QBLK - 1


def _params(**kw):
    return pltpu.CompilerParams(dimension_semantics=("arbitrary",),
                                vmem_limit_bytes=VMEM_LIMIT, **kw)


def _rms(x, g):
    return x * lax.rsqrt(jnp.mean(x * x, axis=-1, keepdims=True) + EPS) * g


def _swap_halves(t):
    lane = lax.broadcasted_iota(jnp.int32, (1, V7X_LANES), 1)
    first_half = (lane % HEAD_DIM) < (HEAD_DIM // 2)
    slabs = []
    for j in range(t.shape[1] // V7X_LANES):
        s = t[:, j * V7X_LANES:(j + 1) * V7X_LANES]
        up = pltpu.roll(s, V7X_LANES - HEAD_DIM // 2, 1)
        down = pltpu.roll(s, HEAD_DIM // 2, 1)
        slabs.append(jnp.where(first_half, up, down))
    return jnp.concatenate(slabs, axis=1)


def _head_norm_rope(t, gain, mean_mat, cos, sin_signed):
    ms = jnp.dot((t * t).astype(BF16), mean_mat, preferred_element_type=F32)
    tn = t * lax.rsqrt(ms + EPS) * gain
    return tn * cos + _swap_halves(tn) * sin_signed


def _full(shape):
    return pl.BlockSpec(shape, lambda i: (0,) * len(shape))


def _layer_block(shape, layer):
    return pl.BlockSpec((None,) + tuple(shape), lambda i: (layer,) + (0,) * len(shape))


def _qkv_pair(proj, g, gq, gk, mean_mat, cos, sin_signed, q_scale=HEAD_DIM ** -0.5):
    c = g * PAIR_WIDTH
    q = _head_norm_rope(proj(OFF_Q + c, PAIR_WIDTH), gq, mean_mat, cos, sin_signed)
    q = q * q_scale
    k = _head_norm_rope(proj(OFF_K + c, PAIR_WIDTH), gk, mean_mat, cos, sin_signed)
    v = proj(OFF_VV + c, PAIR_WIDTH)
    return q, k, v


def _proj_kernel(x_ref, gmix_ref, w_ref, gv_ref, gq_ref, gk_ref, cos_ref, sin_ref, mean_ref,
                 *refs, n_prev):
    prev_kv_refs = refs[:N_PAIRS] if n_prev else (None,) * N_PAIRS
    (u_ref, va_ref, q0_ref, k0_ref, v0_ref, q1_ref, k1_ref, v1_ref, q2_ref, k2_ref, v2_ref,
     kv0_ref, kv1_ref, kv2_ref, scr_ref) = refs[N_PAIRS if n_prev else 0:]
    rows = PROJ_TILE
    qkv_refs = ((q0_ref, k0_ref, v0_ref), (q1_ref, k1_ref, v1_ref), (q2_ref, k2_ref, v2_ref))
    kv_refs = (kv0_ref, kv1_ref, kv2_ref)
    xn = _rms(x_ref[...], gmix_ref[...]).astype(BF16)

    def proj(c0, n):
        return jnp.dot(xn, w_ref[:, c0:c0 + n], preferred_element_type=F32)

    cos = cos_ref[...]
    sin_signed = sin_ref[...]
    mean_mat = mean_ref[...]
    n_slabs = PAIR_WIDTH // V7X_LANES

    def store_residue_major(dst_ref, slot, val, dil):
        for j in range(n_slabs):
            scr_ref[slot, j] = val[:, j * V7X_LANES:(j + 1) * V7X_LANES]
        for r in range(dil):
            picked = [scr_ref[slot, j, pl.ds(r, rows // dil, stride=dil), :]
                      for j in range(n_slabs)]
            dst_ref[0, r] = jnp.concatenate(picked, axis=1).astype(dst_ref.dtype)

    slot = 0
    for g, (window, dil) in reversed(list(enumerate(DILATED_PAIRS))):
        q, k, v = _qkv_pair(proj, g, gq_ref[...], gk_ref[...], mean_mat, cos, sin_signed,
                            q_scale=HEAD_DIM ** -0.5 * LOG2E)
        keep = min(min(window, SEQ), rows)
        if n_prev:
            kv_refs[g][:n_prev] = prev_kv_refs[g][...]
        kv_refs[g][n_prev, 0, :, :PAIR_WIDTH] = k[rows - keep:]
        kv_refs[g][n_prev, 0, :, PAIR_WIDTH:] = v[rows - keep:]
        for ref, val in zip(qkv_refs[g], (q, k, v)):
            if dil == 1:
                ref[...] = val.astype(ref.dtype)
            else:
                store_residue_major(ref, slot, val, dil)
                slot += 1
    va_ref[...] = _rms(jax.nn.gelu(proj(OFF_V, A_WIDTH)), gv_ref[...]).astype(va_ref.dtype)
    u_ref[...] = jax.nn.gelu(proj(OFF_U, A_WIDTH)).astype(u_ref.dtype)


def _proj_prompt(layer, x, p, kv_prev):
    rows = PROJ_TILE
    n_tok = x.shape[0]
    nt = SEQ // rows

    def tok(width):
        return pl.BlockSpec((rows, width), lambda i: (i, 0))

    in_specs = [
        tok(D_MODEL),
        _layer_block((1, D_MODEL), layer),
        _layer_block((D_MODEL, MAIN_COLS), layer),
        _layer_block((1, A_WIDTH), layer),
        _layer_block((1, PAIR_WIDTH), layer),
        _layer_block((1, PAIR_WIDTH), layer),
        pl.BlockSpec((rows, PAIR_WIDTH), lambda i: (i % nt, 0)),
        pl.BlockSpec((rows, PAIR_WIDTH), lambda i: (i % nt, 0)),
        _full((PAIR_WIDTH, PAIR_WIDTH)),
    ]
    out_shape = [jax.ShapeDtypeStruct((n_tok, A_WIDTH), BF16)] * 2
    out_specs = [tok(A_WIDTH)] * 2
    n_strided = 0
    for (window, dil) in DILATED_PAIRS:
        if dil == 1:
            out_shape += [jax.ShapeDtypeStruct((n_tok, PAIR_WIDTH), BF16)] * 3
            out_specs += [tok(PAIR_WIDTH)] * 3
        else:
            n_strided += 3
            out_shape += [jax.ShapeDtypeStruct((BATCH, dil, SEQ // dil, PAIR_WIDTH), BF16)] * 3
            out_specs += [pl.BlockSpec((1, dil, rows // dil, PAIR_WIDTH),
                                       lambda i: (i // nt, 0, i % nt, 0))] * 3
    args = [x, p["g_mix"], p["w_main"], p["g_v"], p["g_q"], p["g_k"], p["cos_p"], p["sin_p"],
            p["mean_mat"]]
    for g, (window, dil) in enumerate(DILATED_PAIRS):
        keep = min(window, SEQ)
        first_tile = (SEQ - max(keep, rows)) // rows

        def kv_spec(n_layers, ft=first_tile, blk=min(keep, rows)):
            return pl.BlockSpec((n_layers, 1, blk, 2 * PAIR_WIDTH),
                                lambda i: (0, i // nt, jnp.maximum(i % nt - ft, 0), 0))

        out_shape.append(jax.ShapeDtypeStruct((layer + 1, BATCH, keep, 2 * PAIR_WIDTH), F32))
        out_specs.append(kv_spec(layer + 1))
        if layer > 0:
            in_specs.append(kv_spec(layer))
            args.append(kv_prev[g])

    return pl.pallas_call(
        functools.partial(_proj_kernel, n_prev=layer),
        grid=(n_tok // rows,),
        in_specs=in_specs,
        out_specs=out_specs,
        out_shape=out_shape,
        scratch_shapes=[pltpu.VMEM((n_strided, PAIR_WIDTH // V7X_LANES, rows, V7X_LANES), F32)],
        compiler_params=_params(),
        name=f"proj_prompt_l{layer}",
    )(*args)


def _attn_block(q, kk, vv, valid):
    nk = kk.shape[0]
    lane_head = lax.broadcasted_iota(jnp.int32, (1, PAIR_WIDTH), 1) // HEAD_DIM
    head_masks = [lane_head == h for h in range(HEADS_PER_PAIR)]
    qs = jnp.concatenate([jnp.where(hm, q, jnp.zeros_like(q)) for hm in head_masks], axis=0)
    s = lax.dot_general(qs, kk, (((1,), (1,)), ((), ())), preferred_element_type=F32)
    s = jnp.where(valid[None], s.reshape(HEADS_PER_PAIR, QBLK, nk), NEG_INF)
    mx = jnp.max(s, axis=-1, keepdims=True)
    p = jnp.exp2(s - mx)
    den = jnp.sum(p, axis=-1, keepdims=True)
    pv = jnp.dot(p.reshape(HEADS_PER_PAIR * QBLK, nk).astype(BF16), vv,
                 preferred_element_type=F32).reshape(HEADS_PER_PAIR, QBLK, PAIR_WIDTH)
    inv_den = 1.0 / den
    lse_h = mx + jnp.log(den) * LOG2E
    o, inv, lse = pv[0], inv_den[0], lse_h[0]
    for h in range(1, HEADS_PER_PAIR):
        o = jnp.where(head_masks[h], pv[h], o)
        inv = jnp.where(head_masks[h], inv_den[h], inv)
        lse = jnp.where(head_masks[h], lse_h[h], lse)
    return o * inv, jnp.broadcast_to(lse, (QBLK, PAIR_WIDTH))


def _attn_kernel(q0_ref, k0_ref, v0_ref, q1_ref, k1_ref, v1_ref, q2_ref, k2_ref, v2_ref,
                 out_ref, o_scr, l_scr):
    qi = lax.broadcasted_iota(jnp.int32, (QBLK, QBLK), 0)
    ki = lax.broadcasted_iota(jnp.int32, (QBLK, QBLK), 1)
    causal = ki <= qi
    qi2 = lax.broadcasted_iota(jnp.int32, (QBLK, 2 * QBLK), 0)
    ki2 = lax.broadcasted_iota(jnp.int32, (QBLK, 2 * QBLK), 1)
    dist = qi2 - ki2 + QBLK
    band = (dist >= 0) & (dist <= TAPS)
    n_slabs = PAIR_WIDTH // V7X_LANES

    def put(scr, g, rows, val):
        for j in range(n_slabs):
            scr[g, j, rows, :] = val[:, j * V7X_LANES:(j + 1) * V7X_LANES]

    def get(scr, g, rows):
        return jnp.concatenate([scr[g, j, rows, :] for j in range(n_slabs)], axis=1)

    for g, refs in enumerate(((q0_ref, k0_ref, v0_ref), (q1_ref, k1_ref, v1_ref),
                              (q2_ref, k2_ref, v2_ref))):
        dil = DILATED_PAIRS[g][1]
        n_blocks = (SEQ // dil) // QBLK

        def rows_of(ref, r, start, dil=dil):
            if dil == 1:
                return ref[pl.ds(start, QBLK), :]
            return ref[0, r, pl.ds(start, QBLK), :]

        def step(idx, carry, g=g, dil=dil, n_blocks=n_blocks, refs=refs, rows_of=rows_of):
            q_ref, k_ref, v_ref = refs
            r, i = idx // n_blocks, idx % n_blocks
            cur = pl.multiple_of(i * QBLK, QBLK)
            q = rows_of(q_ref, r, cur)
            if n_blocks == 1:
                o, lse = _attn_block(q, rows_of(k_ref, r, cur), rows_of(v_ref, r, cur), causal)
            else:
                prev = pl.multiple_of(jnp.maximum(i - 1, 0) * QBLK, QBLK)
                kk = jnp.concatenate([rows_of(k_ref, r, prev), rows_of(k_ref, r, cur)], axis=0)
                vv = jnp.concatenate([rows_of(v_ref, r, prev), rows_of(v_ref, r, cur)], axis=0)
                first_key = jnp.where(i > 0, 0, QBLK)
                o, lse = _attn_block(q, kk, vv, band & (ki2 >= first_key))
            if dil == 1:
                rows = pl.ds(cur, QBLK)
            else:
                rows = pl.ds(r + i * (QBLK * dil), QBLK, stride=dil)
            put(o_scr, g, rows, o)
            put(l_scr, g, rows, lse)
            return carry

        lax.fori_loop(0, dil * n_blocks, step, 0, unroll=ATTN_UNROLL)

    step_rows = 256
    for t0 in range(0, SEQ, step_rows):
        rows = slice(t0, t0 + step_rows)
        l0, l1, l2 = get(l_scr, 0, rows), get(l_scr, 1, rows), get(l_scr, 2, rows)
        m = jnp.maximum(jnp.maximum(l0, l1), l2)
        e0, e1, e2 = jnp.exp2(l0 - m), jnp.exp2(l1 - m), jnp.exp2(l2 - m)
        merged = ((e0 * get(o_scr, 0, rows) + e1 * get(o_scr, 1, rows) + e2 * get(o_scr, 2, rows))
                  * (1.0 / (e0 + e1 + e2)))
        out_ref[rows, :] = merged.astype(out_ref.dtype)


def _attn_prompt(layer, qkv):
    n_tok = qkv[0].shape[0]
    in_specs = []
    for (window, dil) in DILATED_PAIRS:
        if dil == 1:
            in_specs += [pl.BlockSpec((SEQ, PAIR_WIDTH), lambda b: (b, 0))] * 3
        else:
            in_specs += [pl.BlockSpec((1, dil, SEQ // dil, PAIR_WIDTH),
                                      lambda b: (b, 0, 0, 0))] * 3
    scr = pltpu.VMEM((N_PAIRS, PAIR_WIDTH // V7X_LANES, SEQ, V7X_LANES), F32)
    return pl.pallas_call(
        _attn_kernel,
        grid=(BATCH,),
        in_specs=in_specs,
        out_specs=pl.BlockSpec((SEQ, PAIR_WIDTH), lambda b: (b, 0)),
        out_shape=jax.ShapeDtypeStruct((n_tok, PAIR_WIDTH), BF16),
        scratch_shapes=[scr, scr],
        compiler_params=_params(),
        name=f"attn_prompt_l{layer}",
    )(*qkv)


def _gated_merge(x, a_out, b_out, ga, gb, wa_ref, wb_ref, wo_ref):
    ap = jnp.dot(a_out.astype(BF16), wa_ref[...], preferred_element_type=F32)
    bp = jnp.dot(b_out.astype(BF16), wb_ref[...], preferred_element_type=F32)
    merged = ga * ap + gb * bp
    return x + jnp.dot(merged.astype(BF16), wo_ref[...], preferred_element_type=F32)


def _gates(xn, wg_ref):
    ga = jax.nn.sigmoid(jnp.dot(xn, wg_ref[:, :D_MODEL], preferred_element_type=F32))
    gb = jax.nn.sigmoid(jnp.dot(xn, wg_ref[:, D_MODEL:], preferred_element_type=F32))
    return ga, gb


def _mix_kernel(x_ref, u_ref, va_ref, bo_ref, gmix_ref, wg_ref, ws_ref, bias_ref,
                wa_ref, wb_ref, wo_ref, out_ref, a_scr):
    ti = lax.broadcasted_iota(jnp.int32, (CHUNK, CHUNK), 0)
    si = lax.broadcasted_iota(jnp.int32, (CHUNK, CHUNK), 1)
    tril = si <= ti
    w_tril = [jnp.where(tril, ws_ref[g], 0.0).astype(BF16) for g in range(A_GROUPS)]
    low_group = lax.broadcasted_iota(jnp.int32, (1, V7X_LANES), 1) < A_GROUP_DIM
    bias = bias_ref[...]
    groups_per_slab = V7X_LANES // A_GROUP_DIM
    for c in range(TOK_TILE // CHUNK):
        rows = slice(c * CHUNK, (c + 1) * CHUNK)
        zs = []
        for j in range(A_WIDTH // V7X_LANES):
            slab = va_ref[rows, j * V7X_LANES:(j + 1) * V7X_LANES]
            z_lo = jnp.dot(w_tril[groups_per_slab * j], slab, preferred_element_type=F32)
            z_hi = jnp.dot(w_tril[groups_per_slab * j + 1], slab, preferred_element_type=F32)
            zs.append(jnp.where(low_group, z_lo, z_hi))
        z = jnp.concatenate(zs, axis=1) + bias
        a_scr[rows, :] = (u_ref[rows, :].astype(F32) * z).astype(BF16)
    x = x_ref[...]
    ga, gb = _gates(_rms(x, gmix_ref[...]).astype(BF16), wg_ref)
    out_ref[...] = _gated_merge(x, a_scr[...], bo_ref[...], ga, gb, wa_ref, wb_ref, wo_ref)


def _mix_prompt(layer, x, u, va, b_out, p):
    rows = TOK_TILE
    n_tok = x.shape[0]

    def tok(width):
        return pl.BlockSpec((rows, width), lambda i: (i, 0))

    in_specs = [
        tok(D_MODEL), tok(A_WIDTH), tok(A_WIDTH), tok(PAIR_WIDTH),
        _layer_block((1, D_MODEL), layer),
        _layer_block((D_MODEL, 2 * D_MODEL), layer),
        _layer_block((A_GROUPS, CHUNK, CHUNK), layer),
        _layer_block((CHUNK, A_WIDTH), layer),
        _layer_block((A_WIDTH, D_MODEL), layer),
        _layer_block((PAIR_WIDTH, D_MODEL), layer),
        _layer_block((D_MODEL, D_MODEL), layer),
    ]
    return pl.pallas_call(
        _mix_kernel,
        grid=(n_tok // rows,),
        in_specs=in_specs,
        out_specs=tok(D_MODEL),
        out_shape=jax.ShapeDtypeStruct((n_tok, D_MODEL), F32),
        scratch_shapes=[pltpu.VMEM((rows, A_WIDTH), BF16)],
        compiler_params=_params(),
        name=f"mix_prompt_l{layer}",
    )(x, u, va, b_out, p["g_mix"], p["w_gates"], p["w_s"], p["bias_full"],
      p["w_a"], p["w_b"], p["w_o"])


def _ffn_kernel(x_ref, halo_ref, g_ref, wup_ref, cw_ref, cb_ref, wdown_ref,
                out_ref, conv_ref, xn_scr, up_scr, h_scr):
    rows = FFN_TILE
    tile = pl.program_id(0) % (SEQ // FFN_TILE)
    x = x_ref[...]
    g = g_ref[...]
    halo_on = jnp.where(tile == 0, 0.0, 1.0)
    xn_scr[:HALO, :] = (_rms(halo_ref[...], g) * halo_on).astype(BF16)
    xn_scr[HALO:, :] = _rms(x, g).astype(BF16)
    xn = xn_scr[...]
    n_slabs = FF_CHUNK // V7X_LANES

    def up_and_conv(c0, slot):
        cols = slice(c0, c0 + FF_CHUNK)
        up = jnp.dot(xn, wup_ref[:, cols], preferred_element_type=F32)
        conv_ref[0, :, cols] = up[HALO + rows - (CONV_W - 1):]
        for s in range(n_slabs):
            up_scr[slot, s] = up[:, s * V7X_LANES:(s + 1) * V7X_LANES]
        parts = []
        for s in range(n_slabs):
            cs = slice(c0 + s * V7X_LANES, c0 + (s + 1) * V7X_LANES)
            acc = cb_ref[:, cs]
            for j in range(CONV_W - 1):
                lo = HALO - (CONV_W - 1) + j
                acc = acc + cw_ref[j:j + 1, cs] * up_scr[slot, s, lo:lo + rows, :]
            parts.append(acc + cw_ref[CONV_W - 1:CONV_W, cs]
                         * up[HALO:, s * V7X_LANES:(s + 1) * V7X_LANES])
        return jnp.concatenate(parts, axis=1)

    for j in range(N_FF_CHUNKS):
        c_gate = up_and_conv(j * FF_CHUNK, 0)
        c_val = up_and_conv(D_FF + j * FF_CHUNK, 1)
        h_scr[:, j * FF_CHUNK:(j + 1) * FF_CHUNK] = (jax.nn.silu(c_gate) * c_val).astype(BF16)
    out_ref[...] = x + jnp.dot(h_scr[...], wdown_ref[...], preferred_element_type=F32)


def _ffn_prompt(layer, x, p):
    rows = FFN_TILE
    n_tok = x.shape[0]
    nt = SEQ // rows
    halo_per_tile = rows // HALO

    def resident(shape):
        return pl.BlockSpec((None,) + tuple(shape), lambda i: (layer,) + (0,) * len(shape),
                            pipeline_mode=pl.Buffered(1))

    in_specs = [
        pl.BlockSpec((rows, D_MODEL), lambda i: (i, 0)),
        pl.BlockSpec((HALO, D_MODEL), lambda i: (jnp.maximum(i * halo_per_tile - 1, 0), 0)),
        _layer_block((1, D_MODEL), layer),
        resident((D_MODEL, 2 * D_FF)),
        _layer_block((CONV_W, 2 * D_FF), layer),
        _layer_block((1, 2 * D_FF), layer),
        resident((D_FF, D_MODEL)),
    ]
    return pl.pallas_call(
        _ffn_kernel,
        grid=(n_tok // rows,),
        in_specs=in_specs,
        out_specs=[pl.BlockSpec((rows, D_MODEL), lambda i: (i, 0)),
                   pl.BlockSpec((1, CONV_W - 1, 2 * D_FF), lambda i: (i // nt, 0, 0))],
        out_shape=[jax.ShapeDtypeStruct((n_tok, D_MODEL), F32),
                   jax.ShapeDtypeStruct((BATCH, CONV_W - 1, 2 * D_FF), F32)],
        scratch_shapes=[pltpu.VMEM((HALO + rows, D_MODEL), BF16),
                        pltpu.VMEM((2, FF_CHUNK // V7X_LANES, HALO + rows, V7X_LANES), F32),
                        pltpu.VMEM((rows, D_FF), BF16)],
        compiler_params=_params(),
        name=f"ffn_prompt_l{layer}",
    )(x, x, p["g_ffn"], p["w_up"], p["conv_w"], p["conv_b"], p["w_down"])


def _sample_proj_kernel(x_ref, gmix_ref, w_ref, wg_ref, gv_ref, gq_ref, gk_ref, cos_ref, sin_ref,
                        mean_ref, u_ref, va_ref, q_ref, kv_ref, ga_ref, gb_ref):
    xn = _rms(x_ref[...], gmix_ref[...]).astype(BF16)

    def proj(c0, n):
        return jnp.dot(xn, w_ref[:, c0:c0 + n], preferred_element_type=F32)

    u_ref[...] = jax.nn.gelu(proj(OFF_U, A_WIDTH))
    va_ref[...] = _rms(jax.nn.gelu(proj(OFF_V, A_WIDTH)), gv_ref[...])
    ga_ref[...], gb_ref[...] = _gates(xn, wg_ref)
    for g in range(N_PAIRS):
        q, k, v = _qkv_pair(proj, g, gq_ref[...], gk_ref[...], mean_ref[...], cos_ref[...],
                            sin_ref[...])
        q_ref[:, g * PAIR_WIDTH:(g + 1) * PAIR_WIDTH] = q
        kv_ref[:, 2 * g * PAIR_WIDTH:(2 * g + 1) * PAIR_WIDTH] = k
        kv_ref[:, (2 * g + 1) * PAIR_WIDTH:(2 * g + 2) * PAIR_WIDTH] = v


def _proj_sample(layer, x, p):
    rows = x.shape[0]
    in_specs = [
        _full((rows, D_MODEL)),
        _layer_block((1, D_MODEL), layer),
        _layer_block((D_MODEL, MAIN_COLS), layer),
        _layer_block((D_MODEL, 2 * D_MODEL), layer),
        _layer_block((1, A_WIDTH), layer),
        _layer_block((1, PAIR_WIDTH), layer),
        _layer_block((1, PAIR_WIDTH), layer),
        _full((rows, PAIR_WIDTH)),
        _full((rows, PAIR_WIDTH)),
        _full((PAIR_WIDTH, PAIR_WIDTH)),
    ]
    widths = [A_WIDTH, A_WIDTH, B_WIDTH, 2 * B_WIDTH, D_MODEL, D_MODEL]
    return pl.pallas_call(
        _sample_proj_kernel,
        grid=(1,),
        in_specs=in_specs,
        out_specs=[_full((rows, w)) for w in widths],
        out_shape=[jax.ShapeDtypeStruct((rows, w), F32) for w in widths],
        compiler_params=_params(),
        name=f"proj_sample_l{layer}",
    )(x, p["g_mix"], p["w_main"], p["w_gates"], p["g_v"], p["g_q"], p["g_k"],
      p["cos_s"], p["sin_s"], p["mean_mat"])


def _sample_attn_kernel(qt_ref, kvt_ref, c0_ref, c1_ref, c2_ref, sel1_ref, sel2_ref,
                        out_ref, o_scr, l_scr):
    kv_rows = 2 * HEADS_PER_PAIR * HEAD_DIM
    for g, (c_ref, sel_ref) in enumerate(((c0_ref, None), (c1_ref, sel1_ref),
                                          (c2_ref, sel2_ref))):
        window, dil = DILATED_PAIRS[g]
        for b in range(SAMPLE_GROUP):
            col = slice(b, b + 1)
            if sel_ref is not None:
                blk = c_ref[0, b].reshape(kv_rows, window).astype(BF16)
                taps = jnp.dot(blk, sel_ref[...], preferred_element_type=F32)
            for h in range(HEADS_PER_PAIR):
                if sel_ref is None:
                    kt, vt = c_ref[0, b, 0, h], c_ref[0, b, 1, h]
                else:
                    kt = taps[h * HEAD_DIM:(h + 1) * HEAD_DIM]
                    vt = taps[(HEADS_PER_PAIR + h) * HEAD_DIM:(HEADS_PER_PAIR + h + 1) * HEAD_DIM]
                feat = slice(g * PAIR_WIDTH + h * HEAD_DIM, g * PAIR_WIDTH + (h + 1) * HEAD_DIM)
                kfeat = slice(2 * g * PAIR_WIDTH + h * HEAD_DIM,
                              2 * g * PAIR_WIDTH + (h + 1) * HEAD_DIM)
                vfeat = slice((2 * g + 1) * PAIR_WIDTH + h * HEAD_DIM,
                              (2 * g + 1) * PAIR_WIDTH + (h + 1) * HEAD_DIM)
                q = qt_ref[0, feat, col]
                k_new = kvt_ref[0, kfeat, col]
                v_new = kvt_ref[0, vfeat, col]
                s = jnp.sum(kt * q, axis=0, keepdims=True)
                s_new = jnp.sum(k_new * q, axis=0, keepdims=True)
                mx = jnp.maximum(jnp.max(s, axis=1, keepdims=True), s_new)
                pr = jnp.exp(s - mx)
                p_new = jnp.exp(s_new - mx)
                den = jnp.sum(pr, axis=1, keepdims=True) + p_new
                o = (jnp.sum(vt * pr, axis=1, keepdims=True) + p_new * v_new) / den
                out_rows = slice(h * HEAD_DIM, (h + 1) * HEAD_DIM)
                o_scr[g, out_rows, col] = o
                l_scr[g, out_rows, col] = jnp.broadcast_to(mx + jnp.log(den), (HEAD_DIM, 1))
    l0, l1, l2 = l_scr[0], l_scr[1], l_scr[2]
    m = jnp.maximum(jnp.maximum(l0, l1), l2)
    e0, e1, e2 = jnp.exp(l0 - m), jnp.exp(l1 - m), jnp.exp(l2 - m)
    den = e0 + e1 + e2
    out_ref[0] = (e0 / den) * o_scr[0] + (e1 / den) * o_scr[1] + (e2 / den) * o_scr[2]


def _attn_sample(layer, qt, kvt, caches, tap_select):
    n = SAMPLE_GROUP
    groups = qt.shape[0]
    in_specs = [pl.BlockSpec((1, B_WIDTH, n), lambda i: (i, 0, 0)),
                pl.BlockSpec((1, 2 * B_WIDTH, n), lambda i: (i, 0, 0))]
    for (window, dil) in DILATED_PAIRS:
        in_specs.append(pl.BlockSpec((1, n, 2, HEADS_PER_PAIR, HEAD_DIM, window),
                                     lambda i: (layer, i, 0, 0, 0, 0)))
    in_specs += [_full(sel.shape) for sel in tap_select]
    scr = pltpu.VMEM((N_PAIRS, PAIR_WIDTH, n), F32)
    return pl.pallas_call(
        _sample_attn_kernel,
        grid=(groups,),
        in_specs=in_specs,
        out_specs=pl.BlockSpec((1, PAIR_WIDTH, n), lambda i: (i, 0, 0)),
        out_shape=jax.ShapeDtypeStruct((groups, PAIR_WIDTH, n), F32),
        scratch_shapes=[scr, scr],
        compiler_params=_params(),
        name=f"attn_sample_l{layer}",
    )(qt, kvt, *caches, *tap_select)


def _sample_mix_kernel(x_ref, u_ref, va_ref, bo_ref, ga_ref, gb_ref, wdiag_ref, bias0_ref,
                       wa_ref, wb_ref, wo_ref, out_ref):
    a_out = u_ref[...] * (wdiag_ref[...] * va_ref[...] + bias0_ref[...])
    out_ref[...] = _gated_merge(x_ref[...], a_out, bo_ref[...], ga_ref[...], gb_ref[...],
                                wa_ref, wb_ref, wo_ref)


def _mix_sample(layer, x, u, va, b_out, ga, gb, p):
    rows = x.shape[0]
    in_specs = [_full((rows, D_MODEL)), _full((rows, A_WIDTH)), _full((rows, A_WIDTH)),
                _full((rows, PAIR_WIDTH)), _full((rows, D_MODEL)), _full((rows, D_MODEL)),
                _layer_block((1, A_WIDTH), layer), _layer_block((1, A_WIDTH), layer),
                _layer_block((A_WIDTH, D_MODEL), layer),
                _layer_block((PAIR_WIDTH, D_MODEL), layer),
                _layer_block((D_MODEL, D_MODEL), layer)]
    return pl.pallas_call(
        _sample_mix_kernel,
        grid=(1,),
        in_specs=in_specs,
        out_specs=_full((rows, D_MODEL)),
        out_shape=jax.ShapeDtypeStruct((rows, D_MODEL), F32),
        compiler_params=_params(),
        name=f"mix_sample_l{layer}",
    )(x, u, va, b_out, ga, gb, p["wdiag"], p["bias0"], p["w_a"], p["w_b"], p["w_o"])


def _sample_ffn_kernel(x_ref, g_ref, wg_ref, wv_ref, cwg_ref, cwv_ref, cbg_ref, cbv_ref,
                       h0g_ref, h0v_ref, h1g_ref, h1v_ref, wdown_ref,
                       out_ref, upg_ref, upv_ref):
    j = pl.program_id(0)
    x = x_ref[...]
    xn = _rms(x, g_ref[...]).astype(BF16)

    def up_and_conv(w_ref, cw_ref, cb_ref, h0_ref, h1_ref, up_ref):
        up = jnp.dot(xn, w_ref[...], preferred_element_type=F32)
        up_ref[...] = up
        return (cb_ref[...] + cw_ref[0:1, :] * h0_ref[...] + cw_ref[1:2, :] * h1_ref[...]
                + cw_ref[2:3, :] * up)

    c_gate = up_and_conv(wg_ref, cwg_ref, cbg_ref, h0g_ref, h1g_ref, upg_ref)
    c_val = up_and_conv(wv_ref, cwv_ref, cbv_ref, h0v_ref, h1v_ref, upv_ref)
    h = (jax.nn.silu(c_gate) * c_val).astype(BF16)
    part = jnp.dot(h, wdown_ref[...], preferred_element_type=F32)

    @pl.when(j == 0)
    def _():
        out_ref[...] = x + part

    @pl.when(j > 0)
    def _():
        out_ref[...] += part


def _ffn_sample(layer, x, p, state_conv):
    rows = x.shape[0]
    fc = SAMPLE_FF_CHUNK
    nch = D_FF // fc

    def cols(nrows, blk):
        return pl.BlockSpec((None, nrows, fc), lambda j: (layer, 0, blk * nch + j))

    state_flat = state_conv.reshape(DEPTH, rows, (CONV_W - 1) * 2 * D_FF)
    hist = [cols(rows, 2 * row + half) for row in range(CONV_W - 1) for half in range(2)]
    in_specs = [
        _full((rows, D_MODEL)),
        _layer_block((1, D_MODEL), layer),
        cols(D_MODEL, 0), cols(D_MODEL, 1),
        cols(CONV_W, 0), cols(CONV_W, 1),
        cols(1, 0), cols(1, 1),
        *hist,
        pl.BlockSpec((None, fc, D_MODEL), lambda j: (layer, j, 0)),
    ]
    up_spec = pl.BlockSpec((rows, fc), lambda j: (0, j))
    up_shape = jax.ShapeDtypeStruct((rows, D_FF), F32)
    return pl.pallas_call(
        _sample_ffn_kernel,
        grid=(nch,),
        in_specs=in_specs,
        out_specs=[_full((rows, D_MODEL)), up_spec, up_spec],
        out_shape=[jax.ShapeDtypeStruct((rows, D_MODEL), F32), up_shape, up_shape],
        compiler_params=_params(),
        name=f"ffn_sample_l{layer}",
    )(x, p["g_ffn"], p["w_up"], p["w_up"], p["conv_w"], p["conv_w"], p["conv_b"], p["conv_b"],
      state_flat, state_flat, state_flat, state_flat, p["w_down"])


def _sample_layer(layer, xs, p, caches_t, state_conv):
    n = SAMPLE_GROUP
    rows = xs.shape[0]
    u, va, q, kv, ga, gb = _proj_sample(layer, xs, p)

    def columns(a):
        return a.reshape(rows // n, n, a.shape[1]).swapaxes(1, 2)

    bt = _attn_sample(layer, columns(q), columns(kv), caches_t, p["tap_select"])
    b_out = bt.swapaxes(1, 2).reshape(rows, PAIR_WIDTH)
    xs = _mix_sample(layer, xs, u, va, b_out, ga, gb, p)
    xs, upg, upv = _ffn_sample(layer, xs, p, state_conv)
    up_row = jnp.concatenate([upg, upv], axis=-1)[:, None, :]
    new_conv = jnp.concatenate([state_conv[layer][:, 1:], up_row], axis=1)
    return xs, kv, va, new_conv


def _rope_tables(pos):
    half = HEAD_DIM // 2
    inv_freq = ROPE_THETA ** (-jnp.arange(half, dtype=F32) / half)
    ang = pos.astype(F32)[:, None] * inv_freq[None, :]
    cos, sin = jnp.cos(ang), jnp.sin(ang)
    cos_full = jnp.tile(jnp.concatenate([cos, cos], axis=1), (1, HEADS_PER_PAIR))
    sin_signed = jnp.tile(jnp.concatenate([-sin, sin], axis=1), (1, HEADS_PER_PAIR))
    return cos_full, sin_signed


def _prepare_params(g_mix, w_in, g_v, w_s, b_s, g_q, g_k, w_a_proj, w_b_proj, w_o,
                    g_ffn, w_up, conv_w, conv_b, w_down):
    head_of = jnp.arange(PAIR_WIDTH) // HEAD_DIM
    same_head = head_of[:, None] == head_of[None, :]
    bias_full = jnp.repeat(jnp.swapaxes(b_s, 1, 2), A_GROUP_DIM, axis=2)
    cos_p, sin_p = _rope_tables(jnp.arange(SEQ))
    cos_s, sin_s = _rope_tables(jnp.full((DEC_BATCH,), PAST_LEN))
    tap_select = [
        (jnp.arange(window)[:, None] == jnp.arange(TAPS)[None, :] * dil).astype(BF16)
        for (window, dil) in DILATED_PAIRS if dil > 1]
    return dict(
        tap_select=tap_select,
        g_mix=g_mix[:, None, :], g_v=g_v[:, None, :], g_ffn=g_ffn[:, None, :],
        g_q=jnp.tile(g_q, (1, HEADS_PER_PAIR))[:, None, :],
        g_k=jnp.tile(g_k, (1, HEADS_PER_PAIR))[:, None, :],
        w_main=w_in[:, :, :MAIN_COLS].astype(BF16),
        w_gates=w_in[:, :, OFF_GATES:].astype(BF16),
        w_a=w_a_proj.astype(BF16), w_b=w_b_proj.astype(BF16), w_o=w_o.astype(BF16),
        w_up=w_up.astype(BF16), w_down=w_down.astype(BF16),
        w_s=w_s, bias_full=bias_full, bias0=bias_full[:, 0:1, :],
        wdiag=jnp.repeat(w_s[:, :, 0, 0], A_GROUP_DIM, axis=1)[:, None, :],
        conv_w=conv_w, conv_b=conv_b[:, None, :],
        mean_mat=jnp.where(same_head, 1.0 / HEAD_DIM, 0.0).astype(BF16),
        cos_p=cos_p, sin_p=sin_p, cos_s=cos_s, sin_s=sin_s,
    )


def kernel(x_prompt, x_sample, cache_kv_w128, cache_kv_w512, cache_kv_w2048, state_conv,
           g_mix, w_in, g_v, w_s, b_s, g_q, g_k, w_a_proj, w_b_proj, w_o,
           g_ffn, w_up, conv_w, conv_b, w_down):
    assert x_prompt.shape == (BATCH, SEQ, D_MODEL) and x_sample.shape == (DEC_BATCH, 1, D_MODEL)
    p = _prepare_params(g_mix, w_in, g_v, w_s, b_s, g_q, g_k, w_a_proj, w_b_proj, w_o,
                        g_ffn, w_up, conv_w, conv_b, w_down)
    caches_t = []
    for cache, (window, dil) in zip((cache_kv_w128, cache_kv_w512, cache_kv_w2048), DILATED_PAIRS):
        assert cache.shape[2] == window
        caches_t.append(jnp.transpose(cache, (0, 1, 3, 4, 5, 2)))

    xp = x_prompt.reshape(BATCH * SEQ, D_MODEL)
    xs = x_sample.reshape(DEC_BATCH, D_MODEL)
    kv_prompt = None
    conv_p, conv_s, kv_s, va_s = [], [], [], []
    for l in range(DEPTH):
        outs = _proj_prompt(l, xp, p, kv_prompt)
        u, va = outs[0], outs[1]
        qkv = outs[2:11]
        kv_prompt = outs[11:14]
        b_out = _attn_prompt(l, qkv)
        xp = _mix_prompt(l, xp, u, va, b_out, p)
        xp, nconv = _ffn_prompt(l, xp, p)
        conv_p.append(nconv)

        xs, skv, sva, sconv = _sample_layer(l, xs, p, caches_t, state_conv)
        kv_s.append(skv)
        va_s.append(sva)
        conv_s.append(sconv)

    def kv_shape(a, rows):
        return a.reshape(DEPTH, -1, rows, 2, HEADS_PER_PAIR, HEAD_DIM)

    kv_s = jnp.stack(kv_s, axis=0)
    kv_s = [kv_s[:, :, 2 * g * PAIR_WIDTH:(2 * g + 2) * PAIR_WIDTH] for g in range(N_PAIRS)]
    return (
        xp.reshape(BATCH, SEQ, D_MODEL),
        xs.reshape(DEC_BATCH, 1, D_MODEL),
        kv_shape(kv_prompt[0], min(DILATED_PAIRS[0][0], SEQ)),
        kv_shape(kv_prompt[1], min(DILATED_PAIRS[1][0], SEQ)),
        kv_shape(kv_prompt[2], min(DILATED_PAIRS[2][0], SEQ)),
        jnp.stack(conv_p, axis=0),
        kv_shape(kv_s[0], 1),
        kv_shape(kv_s[1], 1),
        kv_shape(kv_s[2], 1),
        jnp.stack(conv_s, axis=0),
        jnp.stack(va_s, axis=0).reshape(DEPTH, DEC_BATCH, 1, A_WIDTH),
    )
```

```python
import functools

import jax
import jax.numpy as jnp
from jax import lax
from jax.experimental import pallas as pl
from jax.experimental.pallas import tpu as pltpu

D_MODEL = 1024
BATCH = 16
SEQ = 2048
DEPTH = 2
DEC_BATCH = 32
PAST_LEN = 16384
CHUNK = 128
A_GROUPS = 8
A_GROUP_DIM = 64
A_WIDTH = A_GROUPS * A_GROUP_DIM
HEAD_DIM = 64
HEADS_PER_PAIR = 4
DILATED_PAIRS = ((128, 1), (512, 4), (2048, 16))
N_PAIRS = len(DILATED_PAIRS)
TAPS = DILATED_PAIRS[0][0] // DILATED_PAIRS[0][1]
PAIR_WIDTH = HEADS_PER_PAIR * HEAD_DIM
B_WIDTH = N_PAIRS * PAIR_WIDTH
QBLK = 128
D_FF = 2816
CONV_W = 3
ROPE_THETA = 10000.0
EPS = 1e-6
NEG_INF = -1e30
LOG2E = 1.4426950408889634
OFF_U, OFF_V = 0, A_WIDTH
OFF_Q = 2 * A_WIDTH
OFF_K = OFF_Q + B_WIDTH
OFF_VV = OFF_K + B_WIDTH
OFF_GATES = OFF_VV + B_WIDTH
MAIN_COLS = OFF_GATES

V7X_LANES = 128
V7X_VMEM_BYTES = 64 * 1024 * 1024
VMEM_LIMIT = V7X_VMEM_BYTES - 8 * 1024 * 1024

TOK_TILE = 1024
TILES_PER_SEQ = SEQ // TOK_TILE
FFN_TILE = 1024
PROJ_TILE = 512
HALO = 16
FF_CHUNK = 256
N_FF_CHUNKS = D_FF // FF_CHUNK
ATTN_UNROLL = 16
SAMPLE_GROUP = 2
SAMPLE_FF_CHUNK = D_FF // 2

F32 = jnp.float32
BF16 = jnp.bfloat16

assert all(w // d == TAPS for (w, d) in DILATED_PAIRS) and TAPS >= QBLK - 1


def _params(**kw):
    return pltpu.CompilerParams(dimension_semantics=("arbitrary",),
                                vmem_limit_bytes=VMEM_LIMIT, **kw)


def _rms(x, g):
    return x * lax.rsqrt(jnp.mean(x * x, axis=-1, keepdims=True) + EPS) * g


def _swap_halves(t):
    lane = lax.broadcasted_iota(jnp.int32, (1, V7X_LANES), 1)
    first_half = (lane % HEAD_DIM) < (HEAD_DIM // 2)
    slabs = []
    for j in range(t.shape[1] // V7X_LANES):
        s = t[:, j * V7X_LANES:(j + 1) * V7X_LANES]
        up = pltpu.roll(s, V7X_LANES - HEAD_DIM // 2, 1)
        down = pltpu.roll(s, HEAD_DIM // 2, 1)
        slabs.append(jnp.where(first_half, up, down))
    return jnp.concatenate(slabs, axis=1)


def _head_norm_rope(t, gain, mean_mat, cos, sin_signed):
    ms = jnp.dot((t * t).astype(BF16), mean_mat, preferred_element_type=F32)
    tn = t * lax.rsqrt(ms + EPS) * gain
    return tn * cos + _swap_halves(tn) * sin_signed


def _full(shape):
    return pl.BlockSpec(shape, lambda i: (0,) * len(shape))


def _layer_block(shape, layer):
    return pl.BlockSpec((None,) + tuple(shape), lambda i: (layer,) + (0,) * len(shape))


def _qkv_pair(proj, g, gq, gk, mean_mat, cos, sin_signed, q_scale=HEAD_DIM ** -0.5):
    c = g * PAIR_WIDTH
    q = _head_norm_rope(proj(OFF_Q + c, PAIR_WIDTH), gq, mean_mat, cos, sin_signed)
    q = q * q_scale
    k = _head_norm_rope(proj(OFF_K + c, PAIR_WIDTH), gk, mean_mat, cos, sin_signed)
    v = proj(OFF_VV + c, PAIR_WIDTH)
    return q, k, v


def _proj_kernel(x_ref, gmix_ref, w_ref, gv_ref, gq_ref, gk_ref, cos_ref, sin_ref, mean_ref,
                 *refs, n_prev):
    prev_kv_refs = refs[:N_PAIRS] if n_prev else (None,) * N_PAIRS
    (u_ref, va_ref, q0_ref, k0_ref, v0_ref, q1_ref, k1_ref, v1_ref, q2_ref, k2_ref, v2_ref,
     kv0_ref, kv1_ref, kv2_ref, scr_ref) = refs[N_PAIRS if n_prev else 0:]
    rows = PROJ_TILE
    qkv_refs = ((q0_ref, k0_ref, v0_ref), (q1_ref, k1_ref, v1_ref), (q2_ref, k2_ref, v2_ref))
    kv_refs = (kv0_ref, kv1_ref, kv2_ref)
    xn = _rms(x_ref[...], gmix_ref[...]).astype(BF16)

    def proj(c0, n):
        return jnp.dot(xn, w_ref[:, c0:c0 + n], preferred_element_type=F32)

    cos = cos_ref[...]
    sin_signed = sin_ref[...]
    mean_mat = mean_ref[...]
    n_slabs = PAIR_WIDTH // V7X_LANES

    def store_residue_major(dst_ref, slot, val, dil):
        for j in range(n_slabs):
            scr_ref[slot, j] = val[:, j * V7X_LANES:(j + 1) * V7X_LANES]
        for r in range(dil):
            picked = [scr_ref[slot, j, pl.ds(r, rows // dil, stride=dil), :]
                      for j in range(n_slabs)]
            dst_ref[0, r] = jnp.concatenate(picked, axis=1).astype(dst_ref.dtype)

    slot = 0
    for g, (window, dil) in reversed(list(enumerate(DILATED_PAIRS))):
        q, k, v = _qkv_pair(proj, g, gq_ref[...], gk_ref[...], mean_mat, cos, sin_signed,
                            q_scale=HEAD_DIM ** -0.5 * LOG2E)
        keep = min(min(window, SEQ), rows)
        if n_prev:
            kv_refs[g][:n_prev] = prev_kv_refs[g][...]
        kv_refs[g][n_prev, 0, :, :PAIR_WIDTH] = k[rows - keep:]
        kv_refs[g][n_prev, 0, :, PAIR_WIDTH:] = v[rows - keep:]
        for ref, val in zip(qkv_refs[g], (q, k, v)):
            if dil == 1:
                ref[...] = val.astype(ref.dtype)
            else:
                store_residue_major(ref, slot, val, dil)
                slot += 1
    va_ref[...] = _rms(jax.nn.gelu(proj(OFF_V, A_WIDTH)), gv_ref[...]).astype(va_ref.dtype)
    u_ref[...] = jax.nn.gelu(proj(OFF_U, A_WIDTH)).astype(u_ref.dtype)


def _proj_prompt(layer, x, p, kv_prev):
    rows = PROJ_TILE
    n_tok = x.shape[0]
    nt = SEQ // rows

    def tok(width):
        return pl.BlockSpec((rows, width), lambda i: (i, 0))

    in_specs = [
        tok(D_MODEL),
        _layer_block((1, D_MODEL), layer),
        _layer_block((D_MODEL, MAIN_COLS), layer),
        _layer_block((1, A_WIDTH), layer),
        _layer_block((1, PAIR_WIDTH), layer),
        _layer_block((1, PAIR_WIDTH), layer),
        pl.BlockSpec((rows, PAIR_WIDTH), lambda i: (i % nt, 0)),
        pl.BlockSpec((rows, PAIR_WIDTH), lambda i: (i % nt, 0)),
        _full((PAIR_WIDTH, PAIR_WIDTH)),
    ]
    out_shape = [jax.ShapeDtypeStruct((n_tok, A_WIDTH), BF16)] * 2
    out_specs = [tok(A_WIDTH)] * 2
    n_strided = 0
    for (window, dil) in DILATED_PAIRS:
        if dil == 1:
            out_shape += [jax.ShapeDtypeStruct((n_tok, PAIR_WIDTH), BF16)] * 3
            out_specs += [tok(PAIR_WIDTH)] * 3
        else:
            n_strided += 3
            out_shape += [jax.ShapeDtypeStruct((BATCH, dil, SEQ // dil, PAIR_WIDTH), BF16)] * 3
            out_specs += [pl.BlockSpec((1, dil, rows // dil, PAIR_WIDTH),
                                       lambda i: (i // nt, 0, i % nt, 0))] * 3
    args = [x, p["g_mix"], p["w_main"], p["g_v"], p["g_q"], p["g_k"], p["cos_p"], p["sin_p"],
            p["mean_mat"]]
    for g, (window, dil) in enumerate(DILATED_PAIRS):
        keep = min(window, SEQ)
        first_tile = (SEQ - max(keep, rows)) // rows

        def kv_spec(n_layers, ft=first_tile, blk=min(keep, rows)):
            return pl.BlockSpec((n_layers, 1, blk, 2 * PAIR_WIDTH),
                                lambda i: (0, i // nt, jnp.maximum(i % nt - ft, 0), 0))

        out_shape.append(jax.ShapeDtypeStruct((layer + 1, BATCH, keep, 2 * PAIR_WIDTH), F32))
        out_specs.append(kv_spec(layer + 1))
        if layer > 0:
            in_specs.append(kv_spec(layer))
            args.append(kv_prev[g])

    return pl.pallas_call(
        functools.partial(_proj_kernel, n_prev=layer),
        grid=(n_tok // rows,),
        in_specs=in_specs,
        out_specs=out_specs,
        out_shape=out_shape,
        scratch_shapes=[pltpu.VMEM((n_strided, PAIR_WIDTH // V7X_LANES, rows, V7X_LANES), F32)],
        compiler_params=_params(),
        name=f"proj_prompt_l{layer}",
    )(*args)


def _attn_block(q, kk, vv, valid):
    nk = kk.shape[0]
    lane_head = lax.broadcasted_iota(jnp.int32, (1, PAIR_WIDTH), 1) // HEAD_DIM
    head_masks = [lane_head == h for h in range(HEADS_PER_PAIR)]
    qs = jnp.concatenate([jnp.where(hm, q, jnp.zeros_like(q)) for hm in head_masks], axis=0)
    s = lax.dot_general(qs, kk, (((1,), (1,)), ((), ())), preferred_element_type=F32)
    s = jnp.where(valid[None], s.reshape(HEADS_PER_PAIR, QBLK, nk), NEG_INF)
    mx = jnp.max(s, axis=-1, keepdims=True)
    p = jnp.exp2(s - mx)
    den = jnp.sum(p, axis=-1, keepdims=True)
    pv = jnp.dot(p.reshape(HEADS_PER_PAIR * QBLK, nk).astype(BF16), vv,
                 preferred_element_type=F32).reshape(HEADS_PER_PAIR, QBLK, PAIR_WIDTH)
    inv_den = 1.0 / den
    lse_h = mx + jnp.log(den) * LOG2E
    o, inv, lse = pv[0], inv_den[0], lse_h[0]
    for h in range(1, HEADS_PER_PAIR):
        o = jnp.where(head_masks[h], pv[h], o)
        inv = jnp.where(head_masks[h], inv_den[h], inv)
        lse = jnp.where(head_masks[h], lse_h[h], lse)
    return o * inv, jnp.broadcast_to(lse, (QBLK, PAIR_WIDTH))


def _attn_kernel(q0_ref, k0_ref, v0_ref, q1_ref, k1_ref, v1_ref, q2_ref, k2_ref, v2_ref,
                 out_ref, o_scr, l_scr):
    qi = lax.broadcasted_iota(jnp.int32, (QBLK, QBLK), 0)
    ki = lax.broadcasted_iota(jnp.int32, (QBLK, QBLK), 1)
    causal = ki <= qi
    qi2 = lax.broadcasted_iota(jnp.int32, (QBLK, 2 * QBLK), 0)
    ki2 = lax.broadcasted_iota(jnp.int32, (QBLK, 2 * QBLK), 1)
    dist = qi2 - ki2 + QBLK
    band = (dist >= 0) & (dist <= TAPS)
    n_slabs = PAIR_WIDTH // V7X_LANES

    def put(scr, g, rows, val):
        for j in range(n_slabs):
            scr[g, j, rows, :] = val[:, j * V7X_LANES:(j + 1) * V7X_LANES]

    def get(scr, g, rows):
        return jnp.concatenate([scr[g, j, rows, :] for j in range(n_slabs)], axis=1)

    for g, refs in enumerate(((q0_ref, k0_ref, v0_ref), (q1_ref, k1_ref, v1_ref),
                              (q2_ref, k2_ref, v2_ref))):
        dil = DILATED_PAIRS[g][1]
        n_blocks = (SEQ // dil) // QBLK

        def rows_of(ref, r, start, dil=dil):
            if dil == 1:
                return ref[pl.ds(start, QBLK), :]
            return ref[0, r, pl.ds(start, QBLK), :]

        def step(idx, carry, g=g, dil=dil, n_blocks=n_blocks, refs=refs, rows_of=rows_of):
            q_ref, k_ref, v_ref = refs
            r, i = idx // n_blocks, idx % n_blocks
            cur = pl.multiple_of(i * QBLK, QBLK)
            q = rows_of(q_ref, r, cur)
            if n_blocks == 1:
                o, lse = _attn_block(q, rows_of(k_ref, r, cur), rows_of(v_ref, r, cur), causal)
            else:
                prev = pl.multiple_of(jnp.maximum(i - 1, 0) * QBLK, QBLK)
                kk = jnp.concatenate([rows_of(k_ref, r, prev), rows_of(k_ref, r, cur)], axis=0)
                vv = jnp.concatenate([rows_of(v_ref, r, prev), rows_of(v_ref, r, cur)], axis=0)
                first_key = jnp.where(i > 0, 0, QBLK)
                o, lse = _attn_block(q, kk, vv, band & (ki2 >= first_key))
            if dil == 1:
                rows = pl.ds(cur, QBLK)
            else:
                rows = pl.ds(r + i * (QBLK * dil), QBLK, stride=dil)
            put(o_scr, g, rows, o)
            put(l_scr, g, rows, lse)
            return carry

        lax.fori_loop(0, dil * n_blocks, step, 0, unroll=ATTN_UNROLL)

    step_rows = 256
    for t0 in range(0, SEQ, step_rows):
        rows = slice(t0, t0 + step_rows)
        l0, l1, l2 = get(l_scr, 0, rows), get(l_scr, 1, rows), get(l_scr, 2, rows)
        m = jnp.maximum(jnp.maximum(l0, l1), l2)
        e0, e1, e2 = jnp.exp2(l0 - m), jnp.exp2(l1 - m), jnp.exp2(l2 - m)
        merged = ((e0 * get(o_scr, 0, rows) + e1 * get(o_scr, 1, rows) + e2 * get(o_scr, 2, rows))
                  * (1.0 / (e0 + e1 + e2)))
        out_ref[rows, :] = merged.astype(out_ref.dtype)


def _attn_prompt(layer, qkv):
    n_tok = qkv[0].shape[0]
    in_specs = []
    for (window, dil) in DILATED_PAIRS:
        if dil == 1:
            in_specs += [pl.BlockSpec((SEQ, PAIR_WIDTH), lambda b: (b, 0))] * 3
        else:
            in_specs += [pl.BlockSpec((1, dil, SEQ // dil, PAIR_WIDTH),
                                      lambda b: (b, 0, 0, 0))] * 3
    scr = pltpu.VMEM((N_PAIRS, PAIR_WIDTH // V7X_LANES, SEQ, V7X_LANES), F32)
    return pl.pallas_call(
        _attn_kernel,
        grid=(BATCH,),
        in_specs=in_specs,
        out_specs=pl.BlockSpec((SEQ, PAIR_WIDTH), lambda b: (b, 0)),
        out_shape=jax.ShapeDtypeStruct((n_tok, PAIR_WIDTH), BF16),
        scratch_shapes=[scr, scr],
        compiler_params=_params(),
        name=f"attn_prompt_l{layer}",
    )(*qkv)


def _gated_merge(x, a_out, b_out, ga, gb, wa_ref, wb_ref, wo_ref):
    ap = jnp.dot(a_out.astype(BF16), wa_ref[...], preferred_element_type=F32)
    bp = jnp.dot(b_out.astype(BF16), wb_ref[...], preferred_element_type=F32)
    merged = ga * ap + gb * bp
    return x + jnp.dot(merged.astype(BF16), wo_ref[...], preferred_element_type=F32)


def _gates(xn, wg_ref):
    ga = jax.nn.sigmoid(jnp.dot(xn, wg_ref[:, :D_MODEL], preferred_element_type=F32))
    gb = jax.nn.sigmoid(jnp.dot(xn, wg_ref[:, D_MODEL:], preferred_element_type=F32))
    return ga, gb


def _mix_kernel(x_ref, u_ref, va_ref, bo_ref, gmix_ref, wg_ref, ws_ref, bias_ref,
                wa_ref, wb_ref, wo_ref, out_ref, a_scr):
    ti = lax.broadcasted_iota(jnp.int32, (CHUNK, CHUNK), 0)
    si = lax.broadcasted_iota(jnp.int32, (CHUNK, CHUNK), 1)
    tril = si <= ti
    w_tril = [jnp.where(tril, ws_ref[g], 0.0).astype(BF16) for g in range(A_GROUPS)]
    low_group = lax.broadcasted_iota(jnp.int32, (1, V7X_LANES), 1) < A_GROUP_DIM
    bias = bias_ref[...]
    groups_per_slab = V7X_LANES // A_GROUP_DIM
    for c in range(TOK_TILE // CHUNK):
        rows = slice(c * CHUNK, (c + 1) * CHUNK)
        zs = []
        for j in range(A_WIDTH // V7X_LANES):
            slab = va_ref[rows, j * V7X_LANES:(j + 1) * V7X_LANES]
            z_lo = jnp.dot(w_tril[groups_per_slab * j], slab, preferred_element_type=F32)
            z_hi = jnp.dot(w_tril[groups_per_slab * j + 1], slab, preferred_element_type=F32)
            zs.append(jnp.where(low_group, z_lo, z_hi))
        z = jnp.concatenate(zs, axis=1) + bias
        a_scr[rows, :] = (u_ref[rows, :].astype(F32) * z).astype(BF16)
    x = x_ref[...]
    ga, gb = _gates(_rms(x, gmix_ref[...]).astype(BF16), wg_ref)
    out_ref[...] = _gated_merge(x, a_scr[...], bo_ref[...], ga, gb, wa_ref, wb_ref, wo_ref)


def _mix_prompt(layer, x, u, va, b_out, p):
    rows = TOK_TILE
    n_tok = x.shape[0]

    def tok(width):
        return pl.BlockSpec((rows, width), lambda i: (i, 0))

    in_specs = [
        tok(D_MODEL), tok(A_WIDTH), tok(A_WIDTH), tok(PAIR_WIDTH),
        _layer_block((1, D_MODEL), layer),
        _layer_block((D_MODEL, 2 * D_MODEL), layer),
        _layer_block((A_GROUPS, CHUNK, CHUNK), layer),
        _layer_block((CHUNK, A_WIDTH), layer),
        _layer_block((A_WIDTH, D_MODEL), layer),
        _layer_block((PAIR_WIDTH, D_MODEL), layer),
        _layer_block((D_MODEL, D_MODEL), layer),
    ]
    return pl.pallas_call(
        _mix_kernel,
        grid=(n_tok // rows,),
        in_specs=in_specs,
        out_specs=tok(D_MODEL),
        out_shape=jax.ShapeDtypeStruct((n_tok, D_MODEL), F32),
        scratch_shapes=[pltpu.VMEM((rows, A_WIDTH), BF16)],
        compiler_params=_params(),
        name=f"mix_prompt_l{layer}",
    )(x, u, va, b_out, p["g_mix"], p["w_gates"], p["w_s"], p["bias_full"],
      p["w_a"], p["w_b"], p["w_o"])


def _ffn_kernel(x_ref, halo_ref, g_ref, wup_ref, cw_ref, cb_ref, wdown_ref,
                out_ref, conv_ref, xn_scr, up_scr, h_scr):
    rows = FFN_TILE
    tile = pl.program_id(0) % (SEQ // FFN_TILE)
    x = x_ref[...]
    g = g_ref[...]
    halo_on = jnp.where(tile == 0, 0.0, 1.0)
    xn_scr[:HALO, :] = (_rms(halo_ref[...], g) * halo_on).astype(BF16)
    xn_scr[HALO:, :] = _rms(x, g).astype(BF16)
    xn = xn_scr[...]
    n_slabs = FF_CHUNK // V7X_LANES

    def up_and_conv(c0, slot):
        cols = slice(c0, c0 + FF_CHUNK)
        up = jnp.dot(xn, wup_ref[:, cols], preferred_element_type=F32)
        conv_ref[0, :, cols] = up[HALO + rows - (CONV_W - 1):]
        for s in range(n_slabs):
            up_scr[slot, s] = up[:, s * V7X_LANES:(s + 1) * V7X_LANES]
        parts = []
        for s in range(n_slabs):
            cs = slice(c0 + s * V7X_LANES, c0 + (s + 1) * V7X_LANES)
            acc = cb_ref[:, cs]
            for j in range(CONV_W - 1):
                lo = HALO - (CONV_W - 1) + j
                acc = acc + cw_ref[j:j + 1, cs] * up_scr[slot, s, lo:lo + rows, :]
            parts.append(acc + cw_ref[CONV_W - 1:CONV_W, cs]
                         * up[HALO:, s * V7X_LANES:(s + 1) * V7X_LANES])
        return jnp.concatenate(parts, axis=1)

    for j in range(N_FF_CHUNKS):
        c_gate = up_and_conv(j * FF_CHUNK, 0)
        c_val = up_and_conv(D_FF + j * FF_CHUNK, 1)
        h_scr[:, j * FF_CHUNK:(j + 1) * FF_CHUNK] = (jax.nn.silu(c_gate) * c_val).astype(BF16)
    out_ref[...] = x + jnp.dot(h_scr[...], wdown_ref[...], preferred_element_type=F32)


def _ffn_prompt(layer, x, p):
    rows = FFN_TILE
    n_tok = x.shape[0]
    nt = SEQ // rows
    halo_per_tile = rows // HALO

    def resident(shape):
        return pl.BlockSpec((None,) + tuple(shape), lambda i: (layer,) + (0,) * len(shape),
                            pipeline_mode=pl.Buffered(1))

    in_specs = [
        pl.BlockSpec((rows, D_MODEL), lambda i: (i, 0)),
        pl.BlockSpec((HALO, D_MODEL), lambda i: (jnp.maximum(i * halo_per_tile - 1, 0), 0)),
        _layer_block((1, D_MODEL), layer),
        resident((D_MODEL, 2 * D_FF)),
        _layer_block((CONV_W, 2 * D_FF), layer),
        _layer_block((1, 2 * D_FF), layer),
        resident((D_FF, D_MODEL)),
    ]
    return pl.pallas_call(
        _ffn_kernel,
        grid=(n_tok // rows,),
        in_specs=in_specs,
        out_specs=[pl.BlockSpec((rows, D_MODEL), lambda i: (i, 0)),
                   pl.BlockSpec((1, CONV_W - 1, 2 * D_FF), lambda i: (i // nt, 0, 0))],
        out_shape=[jax.ShapeDtypeStruct((n_tok, D_MODEL), F32),
                   jax.ShapeDtypeStruct((BATCH, CONV_W - 1, 2 * D_FF), F32)],
        scratch_shapes=[pltpu.VMEM((HALO + rows, D_MODEL), BF16),
                        pltpu.VMEM((2, FF_CHUNK // V7X_LANES, HALO + rows, V7X_LANES), F32),
                        pltpu.VMEM((rows, D_FF), BF16)],
        compiler_params=_params(),
        name=f"ffn_prompt_l{layer}",
    )(x, x, p["g_ffn"], p["w_up"], p["conv_w"], p["conv_b"], p["w_down"])


def _sample_proj_kernel(x_ref, gmix_ref, w_ref, wg_ref, gv_ref, gq_ref, gk_ref, cos_ref, sin_ref,
                        mean_ref, u_ref, va_ref, q_ref, kv_ref, ga_ref, gb_ref):
    xn = _rms(x_ref[...], gmix_ref[...]).astype(BF16)

    def proj(c0, n):
        return jnp.dot(xn, w_ref[:, c0:c0 + n], preferred_element_type=F32)

    u_ref[...] = jax.nn.gelu(proj(OFF_U, A_WIDTH))
    va_ref[...] = _rms(jax.nn.gelu(proj(OFF_V, A_WIDTH)), gv_ref[...])
    ga_ref[...], gb_ref[...] = _gates(xn, wg_ref)
    for g in range(N_PAIRS):
        q, k, v = _qkv_pair(proj, g, gq_ref[...], gk_ref[...], mean_ref[...], cos_ref[...],
                            sin_ref[...])
        q_ref[:, g * PAIR_WIDTH:(g + 1) * PAIR_WIDTH] = q
        kv_ref[:, 2 * g * PAIR_WIDTH:(2 * g + 1) * PAIR_WIDTH] = k
        kv_ref[:, (2 * g + 1) * PAIR_WIDTH:(2 * g + 2) * PAIR_WIDTH] = v


def _proj_sample(layer, x, p):
    rows = x.shape[0]
    in_specs = [
        _full((rows, D_MODEL)),
        _layer_block((1, D_MODEL), layer),
        _layer_block((D_MODEL, MAIN_COLS), layer),
        _layer_block((D_MODEL, 2 * D_MODEL), layer),
        _layer_block((1, A_WIDTH), layer),
        _layer_block((1, PAIR_WIDTH), layer),
        _layer_block((1, PAIR_WIDTH), layer),
        _full((rows, PAIR_WIDTH)),
        _full((rows, PAIR_WIDTH)),
        _full((PAIR_WIDTH, PAIR_WIDTH)),
    ]
    widths = [A_WIDTH, A_WIDTH, B_WIDTH, 2 * B_WIDTH, D_MODEL, D_MODEL]
    return pl.pallas_call(
        _sample_proj_kernel,
        grid=(1,),
        in_specs=in_specs,
        out_specs=[_full((rows, w)) for w in widths],
        out_shape=[jax.ShapeDtypeStruct((rows, w), F32) for w in widths],
        compiler_params=_params(),
        name=f"proj_sample_l{layer}",
    )(x, p["g_mix"], p["w_main"], p["w_gates"], p["g_v"], p["g_q"], p["g_k"],
      p["cos_s"], p["sin_s"], p["mean_mat"])


def _sample_attn_kernel(qt_ref, kvt_ref, c0_ref, c1_ref, c2_ref, sel1_ref, sel2_ref,
                        out_ref, o_scr, l_scr):
    kv_rows = 2 * HEADS_PER_PAIR * HEAD_DIM
    for g, (c_ref, sel_ref) in enumerate(((c0_ref, None), (c1_ref, sel1_ref),
                                          (c2_ref, sel2_ref))):
        window, dil = DILATED_PAIRS[g]
        for b in range(SAMPLE_GROUP):
            col = slice(b, b + 1)
            if sel_ref is not None:
                blk = c_ref[0, b].reshape(kv_rows, window).astype(BF16)
                taps = jnp.dot(blk, sel_ref[...], preferred_element_type=F32)
            for h in range(HEADS_PER_PAIR):
                if sel_ref is None:
                    kt, vt = c_ref[0, b, 0, h], c_ref[0, b, 1, h]
                else:
                    kt = taps[h * HEAD_DIM:(h + 1) * HEAD_DIM]
                    vt = taps[(HEADS_PER_PAIR + h) * HEAD_DIM:(HEADS_PER_PAIR + h + 1) * HEAD_DIM]
                feat = slice(g * PAIR_WIDTH + h * HEAD_DIM, g * PAIR_WIDTH + (h + 1) * HEAD_DIM)
                kfeat = slice(2 * g * PAIR_WIDTH + h * HEAD_DIM,
                              2 * g * PAIR_WIDTH + (h + 1) * HEAD_DIM)
                vfeat = slice((2 * g + 1) * PAIR_WIDTH + h * HEAD_DIM,
                              (2 * g + 1) * PAIR_WIDTH + (h + 1) * HEAD_DIM)
                q = qt_ref[0, feat, col]
                k_new = kvt_ref[0, kfeat, col]
                v_new = kvt_ref[0, vfeat, col]
                s = jnp.sum(kt * q, axis=0, keepdims=True)
                s_new = jnp.sum(k_new * q, axis=0, keepdims=True)
                mx = jnp.maximum(jnp.max(s, axis=1, keepdims=True), s_new)
                pr = jnp.exp(s - mx)
                p_new = jnp.exp(s_new - mx)
                den = jnp.sum(pr, axis=1, keepdims=True) + p_new
                o = (jnp.sum(vt * pr, axis=1, keepdims=True) + p_new * v_new) / den
                out_rows = slice(h * HEAD_DIM, (h + 1) * HEAD_DIM)
                o_scr[g, out_rows, col] = o
                l_scr[g, out_rows, col] = jnp.broadcast_to(mx + jnp.log(den), (HEAD_DIM, 1))
    l0, l1, l2 = l_scr[0], l_scr[1], l_scr[2]
    m = jnp.maximum(jnp.maximum(l0, l1), l2)
    e0, e1, e2 = jnp.exp(l0 - m), jnp.exp(l1 - m), jnp.exp(l2 - m)
    den = e0 + e1 + e2
    out_ref[0] = (e0 / den) * o_scr[0] + (e1 / den) * o_scr[1] + (e2 / den) * o_scr[2]


def _attn_sample(layer, qt, kvt, caches, tap_select):
    n = SAMPLE_GROUP
    groups = qt.shape[0]
    in_specs = [pl.BlockSpec((1, B_WIDTH, n), lambda i: (i, 0, 0)),
                pl.BlockSpec((1, 2 * B_WIDTH, n), lambda i: (i, 0, 0))]
    for (window, dil) in DILATED_PAIRS:
        in_specs.append(pl.BlockSpec((1, n, 2, HEADS_PER_PAIR, HEAD_DIM, window),
                                     lambda i: (layer, i, 0, 0, 0, 0)))
    in_specs += [_full(sel.shape) for sel in tap_select]
    scr = pltpu.VMEM((N_PAIRS, PAIR_WIDTH, n), F32)
    return pl.pallas_call(
        _sample_attn_kernel,
        grid=(groups,),
        in_specs=in_specs,
        out_specs=pl.BlockSpec((1, PAIR_WIDTH, n), lambda i: (i, 0, 0)),
        out_shape=jax.ShapeDtypeStruct((groups, PAIR_WIDTH, n), F32),
        scratch_shapes=[scr, scr],
        compiler_params=_params(),
        name=f"attn_sample_l{layer}",
    )(qt, kvt, *caches, *tap_select)


def _sample_mix_kernel(x_ref, u_ref, va_ref, bo_ref, ga_ref, gb_ref, wdiag_ref, bias0_ref,
                       wa_ref, wb_ref, wo_ref, out_ref):
    a_out = u_ref[...] * (wdiag_ref[...] * va_ref[...] + bias0_ref[...])
    out_ref[...] = _gated_merge(x_ref[...], a_out, bo_ref[...], ga_ref[...], gb_ref[...],
                                wa_ref, wb_ref, wo_ref)


def _mix_sample(layer, x, u, va, b_out, ga, gb, p):
    rows = x.shape[0]
    in_specs = [_full((rows, D_MODEL)), _full((rows, A_WIDTH)), _full((rows, A_WIDTH)),
                _full((rows, PAIR_WIDTH)), _full((rows, D_MODEL)), _full((rows, D_MODEL)),
                _layer_block((1, A_WIDTH), layer), _layer_block((1, A_WIDTH), layer),
                _layer_block((A_WIDTH, D_MODEL), layer),
                _layer_block((PAIR_WIDTH, D_MODEL), layer),
                _layer_block((D_MODEL, D_MODEL), layer)]
    return pl.pallas_call(
        _sample_mix_kernel,
        grid=(1,),
        in_specs=in_specs,
        out_specs=_full((rows, D_MODEL)),
        out_shape=jax.ShapeDtypeStruct((rows, D_MODEL), F32),
        compiler_params=_params(),
        name=f"mix_sample_l{layer}",
    )(x, u, va, b_out, ga, gb, p["wdiag"], p["bias0"], p["w_a"], p["w_b"], p["w_o"])


def _sample_ffn_kernel(x_ref, g_ref, wg_ref, wv_ref, cwg_ref, cwv_ref, cbg_ref, cbv_ref,
                       h0g_ref, h0v_ref, h1g_ref, h1v_ref, wdown_ref,
                       out_ref, upg_ref, upv_ref):
    j = pl.program_id(0)
    x = x_ref[...]
    xn = _rms(x, g_ref[...]).astype(BF16)

    def up_and_conv(w_ref, cw_ref, cb_ref, h0_ref, h1_ref, up_ref):
        up = jnp.dot(xn, w_ref[...], preferred_element_type=F32)
        up_ref[...] = up
        return (cb_ref[...] + cw_ref[0:1, :] * h0_ref[...] + cw_ref[1:2, :] * h1_ref[...]
                + cw_ref[2:3, :] * up)

    c_gate = up_and_conv(wg_ref, cwg_ref, cbg_ref, h0g_ref, h1g_ref, upg_ref)
    c_val = up_and_conv(wv_ref, cwv_ref, cbv_ref, h0v_ref, h1v_ref, upv_ref)
    h = (jax.nn.silu(c_gate) * c_val).astype(BF16)
    part = jnp.dot(h, wdown_ref[...], preferred_element_type=F32)

    @pl.when(j == 0)
    def _():
        out_ref[...] = x + part

    @pl.when(j > 0)
    def _():
        out_ref[...] += part


def _ffn_sample(layer, x, p, state_conv):
    rows = x.shape[0]
    fc = SAMPLE_FF_CHUNK
    nch = D_FF // fc

    def cols(nrows, blk):
        return pl.BlockSpec((None, nrows, fc), lambda j: (layer, 0, blk * nch + j))

    state_flat = state_conv.reshape(DEPTH, rows, (CONV_W - 1) * 2 * D_FF)
    hist = [cols(rows, 2 * row + half) for row in range(CONV_W - 1) for half in range(2)]
    in_specs = [
        _full((rows, D_MODEL)),
        _layer_block((1, D_MODEL), layer),
        cols(D_MODEL, 0), cols(D_MODEL, 1),
        cols(CONV_W, 0), cols(CONV_W, 1),
        cols(1, 0), cols(1, 1),
        *hist,
        pl.BlockSpec((None, fc, D_MODEL), lambda j: (layer, j, 0)),
    ]
    up_spec = pl.BlockSpec((rows, fc), lambda j: (0, j))
    up_shape = jax.ShapeDtypeStruct((rows, D_FF), F32)
    return pl.pallas_call(
        _sample_ffn_kernel,
        grid=(nch,),
        in_specs=in_specs,
        out_specs=[_full((rows, D_MODEL)), up_spec, up_spec],
        out_shape=[jax.ShapeDtypeStruct((rows, D_MODEL), F32), up_shape, up_shape],
        compiler_params=_params(),
        name=f"ffn_sample_l{layer}",
    )(x, p["g_ffn"], p["w_up"], p["w_up"], p["conv_w"], p["conv_w"], p["conv_b"], p["conv_b"],
      state_flat, state_flat, state_flat, state_flat, p["w_down"])


def _sample_layer(layer, xs, p, caches_t, state_conv):
    n = SAMPLE_GROUP
    rows = xs.shape[0]
    u, va, q, kv, ga, gb = _proj_sample(layer, xs, p)

    def columns(a):
        return a.reshape(rows // n, n, a.shape[1]).swapaxes(1, 2)

    bt = _attn_sample(layer, columns(q), columns(kv), caches_t, p["tap_select"])
    b_out = bt.swapaxes(1, 2).reshape(rows, PAIR_WIDTH)
    xs = _mix_sample(layer, xs, u, va, b_out, ga, gb, p)
    xs, upg, upv = _ffn_sample(layer, xs, p, state_conv)
    up_row = jnp.concatenate([upg, upv], axis=-1)[:, None, :]
    new_conv = jnp.concatenate([state_conv[layer][:, 1:], up_row], axis=1)
    return xs, kv, va, new_conv


def _rope_tables(pos):
    half = HEAD_DIM // 2
    inv_freq = ROPE_THETA ** (-jnp.arange(half, dtype=F32) / half)
    ang = pos.astype(F32)[:, None] * inv_freq[None, :]
    cos, sin = jnp.cos(ang), jnp.sin(ang)
    cos_full = jnp.tile(jnp.concatenate([cos, cos], axis=1), (1, HEADS_PER_PAIR))
    sin_signed = jnp.tile(jnp.concatenate([-sin, sin], axis=1), (1, HEADS_PER_PAIR))
    return cos_full, sin_signed


def _prepare_params(g_mix, w_in, g_v, w_s, b_s, g_q, g_k, w_a_proj, w_b_proj, w_o,
                    g_ffn, w_up, conv_w, conv_b, w_down):
    head_of = jnp.arange(PAIR_WIDTH) // HEAD_DIM
    same_head = head_of[:, None] == head_of[None, :]
    bias_full = jnp.repeat(jnp.swapaxes(b_s, 1, 2), A_GROUP_DIM, axis=2)
    cos_p, sin_p = _rope_tables(jnp.arange(SEQ))
    cos_s, sin_s = _rope_tables(jnp.full((DEC_BATCH,), PAST_LEN))
    tap_select = [
        (jnp.arange(window)[:, None] == jnp.arange(TAPS)[None, :] * dil).astype(BF16)
        for (window, dil) in DILATED_PAIRS if dil > 1]
    return dict(
        tap_select=tap_select,
        g_mix=g_mix[:, None, :], g_v=g_v[:, None, :], g_ffn=g_ffn[:, None, :],
        g_q=jnp.tile(g_q, (1, HEADS_PER_PAIR))[:, None, :],
        g_k=jnp.tile(g_k, (1, HEADS_PER_PAIR))[:, None, :],
        w_main=w_in[:, :, :MAIN_COLS].astype(BF16),
        w_gates=w_in[:, :, OFF_GATES:].astype(BF16),
        w_a=w_a_proj.astype(BF16), w_b=w_b_proj.astype(BF16), w_o=w_o.astype(BF16),
        w_up=w_up.astype(BF16), w_down=w_down.astype(BF16),
        w_s=w_s, bias_full=bias_full, bias0=bias_full[:, 0:1, :],
        wdiag=jnp.repeat(w_s[:, :, 0, 0], A_GROUP_DIM, axis=1)[:, None, :],
        conv_w=conv_w, conv_b=conv_b[:, None, :],
        mean_mat=jnp.where(same_head, 1.0 / HEAD_DIM, 0.0).astype(BF16),
        cos_p=cos_p, sin_p=sin_p, cos_s=cos_s, sin_s=sin_s,
    )


def kernel(x_prompt, x_sample, cache_kv_w128, cache_kv_w512, cache_kv_w2048, state_conv,
           g_mix, w_in, g_v, w_s, b_s, g_q, g_k, w_a_proj, w_b_proj, w_o,
           g_ffn, w_up, conv_w, conv_b, w_down):
    assert x_prompt.shape == (BATCH, SEQ, D_MODEL) and x_sample.shape == (DEC_BATCH, 1, D_MODEL)
    p = _prepare_params(g_mix, w_in, g_v, w_s, b_s, g_q, g_k, w_a_proj, w_b_proj, w_o,
                        g_ffn, w_up, conv_w, conv_b, w_down)
    caches_t = []
    for cache, (window, dil) in zip((cache_kv_w128, cache_kv_w512, cache_kv_w2048), DILATED_PAIRS):
        assert cache.shape[2] == window
        caches_t.append(jnp.transpose(cache, (0, 1, 3, 4, 5, 2)))

    xp = x_prompt.reshape(BATCH * SEQ, D_MODEL)
    xs = x_sample.reshape(DEC_BATCH, D_MODEL)
    kv_prompt = None
    conv_p, conv_s, kv_s, va_s = [], [], [], []
    for l in range(DEPTH):
        outs = _proj_prompt(l, xp, p, kv_prompt)
        u, va = outs[0], outs[1]
        qkv = outs[2:11]
        kv_prompt = outs[11:14]
        b_out = _attn_prompt(l, qkv)
        xp = _mix_prompt(l, xp, u, va, b_out, p)
        xp, nconv = _ffn_prompt(l, xp, p)
        conv_p.append(nconv)

        xs, skv, sva, sconv = _sample_layer(l, xs, p, caches_t, state_conv)
        kv_s.append(skv)
        va_s.append(sva)
        conv_s.append(sconv)

    def kv_shape(a, rows):
        return a.reshape(DEPTH, -1, rows, 2, HEADS_PER_PAIR, HEAD_DIM)

    kv_s = jnp.stack(kv_s, axis=0)
    kv_s = [kv_s[:, :, 2 * g * PAIR_WIDTH:(2 * g + 2) * PAIR_WIDTH] for g in range(N_PAIRS)]
    return (
        xp.reshape(BATCH, SEQ, D_MODEL),
        xs.reshape(DEC_BATCH, 1, D_MODEL),
        kv_shape(kv_prompt[0], min(DILATED_PAIRS[0][0], SEQ)),
        kv_shape(kv_prompt[1], min(DILATED_PAIRS[1][0], SEQ)),
        kv_shape(kv_prompt[2], min(DILATED_PAIRS[2][0], SEQ)),
        jnp.stack(conv_p, axis=0),
        kv_shape(kv_s[0], 1),
        kv_shape(kv_s[1], 1),
        kv_shape(kv_s[2], 1),
        jnp.stack(conv_s, axis=0),
        jnp.stack(va_s, axis=0).reshape(DEPTH, DEC_BATCH, 1, A_WIDTH),
    )
```

```python
import functools

import jax
import jax.numpy as jnp
from jax import lax
from jax.experimental import pallas as pl
from jax.experimental.pallas import tpu as pltpu

D_MODEL = 1024
BATCH = 16
SEQ = 2048
DEPTH = 2
DEC_BATCH = 32
PAST_LEN = 16384
CHUNK = 128
A_GROUPS = 8
A_GROUP_DIM = 64
A_WIDTH = A_GROUPS * A_GROUP_DIM
HEAD_DIM = 64
HEADS_PER_PAIR = 4
DILATED_PAIRS = ((128, 1), (512, 4), (2048, 16))
N_PAIRS = len(DILATED_PAIRS)
TAPS = DILATED_PAIRS[0][0] // DILATED_PAIRS[0][1]
PAIR_WIDTH = HEADS_PER_PAIR * HEAD_DIM
B_WIDTH = N_PAIRS * PAIR_WIDTH
QBLK = 128
D_FF = 2816
CONV_W = 3
ROPE_THETA = 10000.0
EPS = 1e-6
NEG_INF = -1e30
LOG2E = 1.4426950408889634
OFF_U, OFF_V = 0, A_WIDTH
OFF_Q = 2 * A_WIDTH
OFF_K = OFF_Q + B_WIDTH
OFF_VV = OFF_K + B_WIDTH
OFF_GATES = OFF_VV + B_WIDTH
MAIN_COLS = OFF_GATES

V7X_LANES = 128
V7X_VMEM_BYTES = 64 * 1024 * 1024
VMEM_LIMIT = V7X_VMEM_BYTES - 8 * 1024 * 1024

TOK_TILE = 1024
TILES_PER_SEQ = SEQ // TOK_TILE
FFN_TILE = 1024
PROJ_TILE = 512
HALO = 16
FF_CHUNK = 256
N_FF_CHUNKS = D_FF // FF_CHUNK
ATTN_ROWS = 64
ATTN_UNROLL = 16
SAMPLE_GROUP = 2
SAMPLE_FF_CHUNK = D_FF // 2

F32 = jnp.float32
BF16 = jnp.bfloat16

assert all(w // d == TAPS for (w, d) in DILATED_PAIRS) and TAPS >= QBLK - 1


def _params(**kw):
    return pltpu.CompilerParams(dimension_semantics=("arbitrary",),
                                vmem_limit_bytes=VMEM_LIMIT, **kw)


def _rms(x, g):
    return x * lax.rsqrt(jnp.mean(x * x, axis=-1, keepdims=True) + EPS) * g


def _swap_halves(t):
    lane = lax.broadcasted_iota(jnp.int32, (1, V7X_LANES), 1)
    first_half = (lane % HEAD_DIM) < (HEAD_DIM // 2)
    slabs = []
    for j in range(t.shape[1] // V7X_LANES):
        s = t[:, j * V7X_LANES:(j + 1) * V7X_LANES]
        up = pltpu.roll(s, V7X_LANES - HEAD_DIM // 2, 1)
        down = pltpu.roll(s, HEAD_DIM // 2, 1)
        slabs.append(jnp.where(first_half, up, down))
    return jnp.concatenate(slabs, axis=1)


def _head_norm_rope(t, gain, mean_mat, cos, sin_signed):
    ms = jnp.dot((t * t).astype(BF16), mean_mat, preferred_element_type=F32)
    tn = t * lax.rsqrt(ms + EPS) * gain
    return tn * cos + _swap_halves(tn) * sin_signed


def _full(shape):
    return pl.BlockSpec(shape, lambda i: (0,) * len(shape))


def _layer_block(shape, layer):
    return pl.BlockSpec((None,) + tuple(shape), lambda i: (layer,) + (0,) * len(shape))


def _qkv_pair(proj, g, gq, gk, mean_mat, cos, sin_signed, q_scale=HEAD_DIM ** -0.5):
    c = g * PAIR_WIDTH
    q = _head_norm_rope(proj(OFF_Q + c, PAIR_WIDTH), gq, mean_mat, cos, sin_signed)
    q = q * q_scale
    k = _head_norm_rope(proj(OFF_K + c, PAIR_WIDTH), gk, mean_mat, cos, sin_signed)
    v = proj(OFF_VV + c, PAIR_WIDTH)
    return q, k, v


def _proj_kernel(x_ref, gmix_ref, w_ref, gv_ref, gq_ref, gk_ref, cos_ref, sin_ref, mean_ref,
                 *refs, n_prev):
    prev_kv_refs = refs[:N_PAIRS] if n_prev else (None,) * N_PAIRS
    (u_ref, va_ref, q0_ref, k0_ref, v0_ref, q1_ref, k1_ref, v1_ref, q2_ref, k2_ref, v2_ref,
     kv0_ref, kv1_ref, kv2_ref, scr_ref) = refs[N_PAIRS if n_prev else 0:]
    rows = PROJ_TILE
    qkv_refs = ((q0_ref, k0_ref, v0_ref), (q1_ref, k1_ref, v1_ref), (q2_ref, k2_ref, v2_ref))
    kv_refs = (kv0_ref, kv1_ref, kv2_ref)
    xn = _rms(x_ref[...], gmix_ref[...]).astype(BF16)

    def proj(c0, n):
        return jnp.dot(xn, w_ref[:, c0:c0 + n], preferred_element_type=F32)

    cos = cos_ref[...]
    sin_signed = sin_ref[...]
    mean_mat = mean_ref[...]
    n_slabs = PAIR_WIDTH // V7X_LANES

    def store_residue_major(dst_ref, slot, val, dil):
        for j in range(n_slabs):
            scr_ref[slot, j] = val[:, j * V7X_LANES:(j + 1) * V7X_LANES]
        for r in range(dil):
            picked = [scr_ref[slot, j, pl.ds(r, rows // dil, stride=dil), :]
                      for j in range(n_slabs)]
            dst_ref[0, r] = jnp.concatenate(picked, axis=1).astype(dst_ref.dtype)

    slot = 0
    for g, (window, dil) in reversed(list(enumerate(DILATED_PAIRS))):
        q, k, v = _qkv_pair(proj, g, gq_ref[...], gk_ref[...], mean_mat, cos, sin_signed,
                            q_scale=HEAD_DIM ** -0.5 * LOG2E)
        keep = min(min(window, SEQ), rows)
        if n_prev:
            kv_refs[g][:n_prev] = prev_kv_refs[g][...]
        kv_refs[g][n_prev, 0, :, :PAIR_WIDTH] = k[rows - keep:]
        kv_refs[g][n_prev, 0, :, PAIR_WIDTH:] = v[rows - keep:]
        for ref, val in zip(qkv_refs[g], (q, k, v)):
            if dil == 1:
                ref[...] = val.astype(ref.dtype)
            else:
                store_residue_major(ref, slot, val, dil)
                slot += 1
    va_ref[...] = _rms(jax.nn.gelu(proj(OFF_V, A_WIDTH)), gv_ref[...]).astype(va_ref.dtype)
    u_ref[...] = jax.nn.gelu(proj(OFF_U, A_WIDTH)).astype(u_ref.dtype)


def _proj_prompt(layer, x, p, kv_prev):
    rows = PROJ_TILE
    n_tok = x.shape[0]
    nt = SEQ // rows

    def tok(width):
        return pl.BlockSpec((rows, width), lambda i: (i, 0))

    in_specs = [
        tok(D_MODEL),
        _layer_block((1, D_MODEL), layer),
        _layer_block((D_MODEL, MAIN_COLS), layer),
        _layer_block((1, A_WIDTH), layer),
        _layer_block((1, PAIR_WIDTH), layer),
        _layer_block((1, PAIR_WIDTH), layer),
        pl.BlockSpec((rows, PAIR_WIDTH), lambda i: (i % nt, 0)),
        pl.BlockSpec((rows, PAIR_WIDTH), lambda i: (i % nt, 0)),
        _full((PAIR_WIDTH, PAIR_WIDTH)),
    ]
    out_shape = [jax.ShapeDtypeStruct((n_tok, A_WIDTH), BF16)] * 2
    out_specs = [tok(A_WIDTH)] * 2
    n_strided = 0
    for (window, dil) in DILATED_PAIRS:
        if dil == 1:
            out_shape += [jax.ShapeDtypeStruct((n_tok, PAIR_WIDTH), BF16)] * 3
            out_specs += [tok(PAIR_WIDTH)] * 3
        else:
            n_strided += 3
            out_shape += [jax.ShapeDtypeStruct((BATCH, dil, SEQ // dil, PAIR_WIDTH), BF16)] * 3
            out_specs += [pl.BlockSpec((1, dil, rows // dil, PAIR_WIDTH),
                                       lambda i: (i // nt, 0, i % nt, 0))] * 3
    args = [x, p["g_mix"], p["w_main"], p["g_v"], p["g_q"], p["g_k"], p["cos_p"], p["sin_p"],
            p["mean_mat"]]
    for g, (window, dil) in enumerate(DILATED_PAIRS):
        keep = min(window, SEQ)
        first_tile = (SEQ - max(keep, rows)) // rows

        def kv_spec(n_layers, ft=first_tile, blk=min(keep, rows)):
            return pl.BlockSpec((n_layers, 1, blk, 2 * PAIR_WIDTH),
                                lambda i: (0, i // nt, jnp.maximum(i % nt - ft, 0), 0))

        out_shape.append(jax.ShapeDtypeStruct((layer + 1, BATCH, keep, 2 * PAIR_WIDTH), F32))
        out_specs.append(kv_spec(layer + 1))
        if layer > 0:
            in_specs.append(kv_spec(layer))
            args.append(kv_prev[g])

    return pl.pallas_call(
        functools.partial(_proj_kernel, n_prev=layer),
        grid=(n_tok // rows,),
        in_specs=in_specs,
        out_specs=out_specs,
        out_shape=out_shape,
        scratch_shapes=[pltpu.VMEM((n_strided, PAIR_WIDTH // V7X_LANES, rows, V7X_LANES), F32)],
        compiler_params=_params(),
        name=f"proj_prompt_l{layer}",
    )(*args)


def _attn_block(q, kk, vv, valid):
    nk = kk.shape[0]
    lane_head = lax.broadcasted_iota(jnp.int32, (1, PAIR_WIDTH), 1) // HEAD_DIM
    head_masks = [lane_head == h for h in range(HEADS_PER_PAIR)]
    bias = jnp.where(valid, 0.0, NEG_INF)
    outs, lses = [], []
    for r0 in range(0, QBLK, ATTN_ROWS):
        rows = slice(r0, r0 + ATTN_ROWS)
        qr = q[rows]
        qs = jnp.concatenate([jnp.where(hm, qr, jnp.zeros_like(qr)) for hm in head_masks], axis=0)
        s = lax.dot_general(qs, kk, (((1,), (1,)), ((), ())), preferred_element_type=F32)
        s = s.reshape(HEADS_PER_PAIR, ATTN_ROWS, nk) + bias[rows][None]
        mx = jnp.max(s, axis=-1, keepdims=True)
        p = jnp.exp2(s - mx)
        den = jnp.sum(p, axis=-1, keepdims=True)
        pv = jnp.dot(p.reshape(HEADS_PER_PAIR * ATTN_ROWS, nk).astype(BF16), vv,
                     preferred_element_type=F32).reshape(HEADS_PER_PAIR, ATTN_ROWS, PAIR_WIDTH)
        inv_den = 1.0 / den
        lse_h = mx + jnp.log(den) * LOG2E
        o, inv, lse = pv[0], inv_den[0], lse_h[0]
        for h in range(1, HEADS_PER_PAIR):
            o = jnp.where(head_masks[h], pv[h], o)
            inv = jnp.where(head_masks[h], inv_den[h], inv)
            lse = jnp.where(head_masks[h], lse_h[h], lse)
        outs.append(o * inv)
        lses.append(jnp.broadcast_to(lse, (ATTN_ROWS, PAIR_WIDTH)))
    return jnp.concatenate(outs, axis=0), jnp.concatenate(lses, axis=0)


def _attn_kernel(q0_ref, k0_ref, v0_ref, q1_ref, k1_ref, v1_ref, q2_ref, k2_ref, v2_ref,
                 out_ref, o_scr, l_scr):
    qi = lax.broadcasted_iota(jnp.int32, (QBLK, QBLK), 0)
    ki = lax.broadcasted_iota(jnp.int32, (QBLK, QBLK), 1)
    causal = ki <= qi
    qi2 = lax.broadcasted_iota(jnp.int32, (QBLK, 2 * QBLK), 0)
    ki2 = lax.broadcasted_iota(jnp.int32, (QBLK, 2 * QBLK), 1)
    dist = qi2 - ki2 + QBLK
    band = (dist >= 0) & (dist <= TAPS)
    n_slabs = PAIR_WIDTH // V7X_LANES

    def put(scr, g, rows, val):
        for j in range(n_slabs):
            scr[g, j, rows, :] = val[:, j * V7X_LANES:(j + 1) * V7X_LANES]

    def get(scr, g, rows):
        return jnp.concatenate([scr[g, j, rows, :] for j in range(n_slabs)], axis=1)

    for g, refs in enumerate(((q0_ref, k0_ref, v0_ref), (q1_ref, k1_ref, v1_ref),
                              (q2_ref, k2_ref, v2_ref))):
        dil = DILATED_PAIRS[g][1]
        n_blocks = (SEQ // dil) // QBLK

        def rows_of(ref, r, start, dil=dil):
            if dil == 1:
                return ref[pl.ds(start, QBLK), :]
            return ref[0, r, pl.ds(start, QBLK), :]

        def step(idx, carry, g=g, dil=dil, n_blocks=n_blocks, refs=refs, rows_of=rows_of):
            q_ref, k_ref, v_ref = refs
            r, i = idx // n_blocks, idx % n_blocks
            cur = pl.multiple_of(i * QBLK, QBLK)
            q = rows_of(q_ref, r, cur)
            if n_blocks == 1:
                o, lse = _attn_block(q, rows_of(k_ref, r, cur), rows_of(v_ref, r, cur), causal)
            else:
                prev = pl.multiple_of(jnp.maximum(i - 1, 0) * QBLK, QBLK)
                kk = jnp.concatenate([rows_of(k_ref, r, prev), rows_of(k_ref, r, cur)], axis=0)
                vv = jnp.concatenate([rows_of(v_ref, r, prev), rows_of(v_ref, r, cur)], axis=0)
                first_key = jnp.where(i > 0, 0, QBLK)
                o, lse = _attn_block(q, kk, vv, band & (ki2 >= first_key))
            if dil == 1:
                rows = pl.ds(cur, QBLK)
            else:
                rows = pl.ds(r + i * (QBLK * dil), QBLK, stride=dil)
            put(o_scr, g, rows, o)
            put(l_scr, g, rows, lse)
            return carry

        lax.fori_loop(0, dil * n_blocks, step, 0, unroll=ATTN_UNROLL)

    step_rows = 256
    for t0 in range(0, SEQ, step_rows):
        rows = slice(t0, t0 + step_rows)
        l0, l1, l2 = get(l_scr, 0, rows), get(l_scr, 1, rows), get(l_scr, 2, rows)
        m = jnp.maximum(jnp.maximum(l0, l1), l2)
        e0, e1, e2 = jnp.exp2(l0 - m), jnp.exp2(l1 - m), jnp.exp2(l2 - m)
        merged = ((e0 * get(o_scr, 0, rows) + e1 * get(o_scr, 1, rows) + e2 * get(o_scr, 2, rows))
                  * (1.0 / (e0 + e1 + e2)))
        out_ref[rows, :] = merged.astype(out_ref.dtype)


def _attn_prompt(layer, qkv):
    n_tok = qkv[0].shape[0]
    in_specs = []
    for (window, dil) in DILATED_PAIRS:
        if dil == 1:
            in_specs += [pl.BlockSpec((SEQ, PAIR_WIDTH), lambda b: (b, 0))] * 3
        else:
            in_specs += [pl.BlockSpec((1, dil, SEQ // dil, PAIR_WIDTH),
                                      lambda b: (b, 0, 0, 0))] * 3
    scr = pltpu.VMEM((N_PAIRS, PAIR_WIDTH // V7X_LANES, SEQ, V7X_LANES), F32)
    return pl.pallas_call(
        _attn_kernel,
        grid=(BATCH,),
        in_specs=in_specs,
        out_specs=pl.BlockSpec((SEQ, PAIR_WIDTH), lambda b: (b, 0)),
        out_shape=jax.ShapeDtypeStruct((n_tok, PAIR_WIDTH), BF16),
        scratch_shapes=[scr, scr],
        compiler_params=_params(),
        name=f"attn_prompt_l{layer}",
    )(*qkv)


def _gated_merge(x, a_out, b_out, ga, gb, wa_ref, wb_ref, wo_ref):
    ap = jnp.dot(a_out.astype(BF16), wa_ref[...], preferred_element_type=F32)
    bp = jnp.dot(b_out.astype(BF16), wb_ref[...], preferred_element_type=F32)
    merged = ga * ap + gb * bp
    return x + jnp.dot(merged.astype(BF16), wo_ref[...], preferred_element_type=F32)


def _gates(xn, wg_ref):
    ga = jax.nn.sigmoid(jnp.dot(xn, wg_ref[:, :D_MODEL], preferred_element_type=F32))
    gb = jax.nn.sigmoid(jnp.dot(xn, wg_ref[:, D_MODEL:], preferred_element_type=F32))
    return ga, gb


def _mix_kernel(x_ref, u_ref, va_ref, bo_ref, gmix_ref, wg_ref, ws_ref, bias_ref,
                wa_ref, wb_ref, wo_ref, out_ref, a_scr):
    ti = lax.broadcasted_iota(jnp.int32, (CHUNK, CHUNK), 0)
    si = lax.broadcasted_iota(jnp.int32, (CHUNK, CHUNK), 1)
    tril = si <= ti
    w_tril = [jnp.where(tril, ws_ref[g], 0.0).astype(BF16) for g in range(A_GROUPS)]
    low_group = lax.broadcasted_iota(jnp.int32, (1, V7X_LANES), 1) < A_GROUP_DIM
    bias = bias_ref[...]
    groups_per_slab = V7X_LANES // A_GROUP_DIM
    for c in range(TOK_TILE // CHUNK):
        rows = slice(c * CHUNK, (c + 1) * CHUNK)
        zs = []
        for j in range(A_WIDTH // V7X_LANES):
            slab = va_ref[rows, j * V7X_LANES:(j + 1) * V7X_LANES]
            z_lo = jnp.dot(w_tril[groups_per_slab * j], slab, preferred_element_type=F32)
            z_hi = jnp.dot(w_tril[groups_per_slab * j + 1], slab, preferred_element_type=F32)
            zs.append(jnp.where(low_group, z_lo, z_hi))
        z = jnp.concatenate(zs, axis=1) + bias
        a_scr[rows, :] = (u_ref[rows, :].astype(F32) * z).astype(BF16)
    x = x_ref[...]
    ga, gb = _gates(_rms(x, gmix_ref[...]).astype(BF16), wg_ref)
    out_ref[...] = _gated_merge(x, a_scr[...], bo_ref[...], ga, gb, wa_ref, wb_ref, wo_ref)


def _mix_prompt(layer, x, u, va, b_out, p):
    rows = TOK_TILE
    n_tok = x.shape[0]

    def tok(width):
        return pl.BlockSpec((rows, width), lambda i: (i, 0))

    in_specs = [
        tok(D_MODEL), tok(A_WIDTH), tok(A_WIDTH), tok(PAIR_WIDTH),
        _layer_block((1, D_MODEL), layer),
        _layer_block((D_MODEL, 2 * D_MODEL), layer),
        _layer_block((A_GROUPS, CHUNK, CHUNK), layer),
        _layer_block((CHUNK, A_WIDTH), layer),
        _layer_block((A_WIDTH, D_MODEL), layer),
        _layer_block((PAIR_WIDTH, D_MODEL), layer),
        _layer_block((D_MODEL, D_MODEL), layer),
    ]
    return pl.pallas_call(
        _mix_kernel,
        grid=(n_tok // rows,),
        in_specs=in_specs,
        out_specs=tok(D_MODEL),
        out_shape=jax.ShapeDtypeStruct((n_tok, D_MODEL), F32),
        scratch_shapes=[pltpu.VMEM((rows, A_WIDTH), BF16)],
        compiler_params=_params(),
        name=f"mix_prompt_l{layer}",
    )(x, u, va, b_out, p["g_mix"], p["w_gates"], p["w_s"], p["bias_full"],
      p["w_a"], p["w_b"], p["w_o"])


def _ffn_kernel(x_ref, halo_ref, g_ref, wup_ref, cw_ref, cb_ref, wdown_ref,
                out_ref, conv_ref, xn_scr, up_scr, h_scr):
    rows = FFN_TILE
    tile = pl.program_id(0) % (SEQ // FFN_TILE)
    x = x_ref[...]
    g = g_ref[...]
    halo_on = jnp.where(tile == 0, 0.0, 1.0)
    xn_scr[:HALO, :] = (_rms(halo_ref[...], g) * halo_on).astype(BF16)
    xn_scr[HALO:, :] = _rms(x, g).astype(BF16)
    xn = xn_scr[...]
    n_slabs = FF_CHUNK // V7X_LANES

    def up_and_conv(c0, slot):
        cols = slice(c0, c0 + FF_CHUNK)
        up = jnp.dot(xn, wup_ref[:, cols], preferred_element_type=F32)
        conv_ref[0, :, cols] = up[HALO + rows - (CONV_W - 1):]
        for s in range(n_slabs):
            up_scr[slot, s] = up[:, s * V7X_LANES:(s + 1) * V7X_LANES]
        parts = []
        for s in range(n_slabs):
            cs = slice(c0 + s * V7X_LANES, c0 + (s + 1) * V7X_LANES)
            acc = cb_ref[:, cs]
            for j in range(CONV_W - 1):
                lo = HALO - (CONV_W - 1) + j
                acc = acc + cw_ref[j:j + 1, cs] * up_scr[slot, s, lo:lo + rows, :]
            parts.append(acc + cw_ref[CONV_W - 1:CONV_W, cs]
                         * up[HALO:, s * V7X_LANES:(s + 1) * V7X_LANES])
        return jnp.concatenate(parts, axis=1)

    for j in range(N_FF_CHUNKS):
        c_gate = up_and_conv(j * FF_CHUNK, 0)
        c_val = up_and_conv(D_FF + j * FF_CHUNK, 1)
        h_scr[:, j * FF_CHUNK:(j + 1) * FF_CHUNK] = (jax.nn.silu(c_gate) * c_val).astype(BF16)
    out_ref[...] = x + jnp.dot(h_scr[...], wdown_ref[...], preferred_element_type=F32)


def _ffn_prompt(layer, x, p):
    rows = FFN_TILE
    n_tok = x.shape[0]
    nt = SEQ // rows
    halo_per_tile = rows // HALO

    def resident(shape):
        return pl.BlockSpec((None,) + tuple(shape), lambda i: (layer,) + (0,) * len(shape),
                            pipeline_mode=pl.Buffered(1))

    in_specs = [
        pl.BlockSpec((rows, D_MODEL), lambda i: (i, 0)),
        pl.BlockSpec((HALO, D_MODEL), lambda i: (jnp.maximum(i * halo_per_tile - 1, 0), 0)),
        _layer_block((1, D_MODEL), layer),
        resident((D_MODEL, 2 * D_FF)),
        _layer_block((CONV_W, 2 * D_FF), layer),
        _layer_block((1, 2 * D_FF), layer),
        resident((D_FF, D_MODEL)),
    ]
    return pl.pallas_call(
        _ffn_kernel,
        grid=(n_tok // rows,),
        in_specs=in_specs,
        out_specs=[pl.BlockSpec((rows, D_MODEL), lambda i: (i, 0)),
                   pl.BlockSpec((1, CONV_W - 1, 2 * D_FF), lambda i: (i // nt, 0, 0))],
        out_shape=[jax.ShapeDtypeStruct((n_tok, D_MODEL), F32),
                   jax.ShapeDtypeStruct((BATCH, CONV_W - 1, 2 * D_FF), F32)],
        scratch_shapes=[pltpu.VMEM((HALO + rows, D_MODEL), BF16),
                        pltpu.VMEM((2, FF_CHUNK // V7X_LANES, HALO + rows, V7X_LANES), F32),
                        pltpu.VMEM((rows, D_FF), BF16)],
        compiler_params=_params(),
        name=f"ffn_prompt_l{layer}",
    )(x, x, p["g_ffn"], p["w_up"], p["conv_w"], p["conv_b"], p["w_down"])


def _sample_proj_kernel(x_ref, gmix_ref, w_ref, wg_ref, gv_ref, gq_ref, gk_ref, cos_ref, sin_ref,
                        mean_ref, u_ref, va_ref, q_ref, kv_ref, ga_ref, gb_ref):
    xn = _rms(x_ref[...], gmix_ref[...]).astype(BF16)

    def proj(c0, n):
        return jnp.dot(xn, w_ref[:, c0:c0 + n], preferred_element_type=F32)

    u_ref[...] = jax.nn.gelu(proj(OFF_U, A_WIDTH))
    va_ref[...] = _rms(jax.nn.gelu(proj(OFF_V, A_WIDTH)), gv_ref[...])
    ga_ref[...], gb_ref[...] = _gates(xn, wg_ref)
    for g in range(N_PAIRS):
        q, k, v = _qkv_pair(proj, g, gq_ref[...], gk_ref[...], mean_ref[...], cos_ref[...],
                            sin_ref[...])
        q_ref[:, g * PAIR_WIDTH:(g + 1) * PAIR_WIDTH] = q
        kv_ref[:, 2 * g * PAIR_WIDTH:(2 * g + 1) * PAIR_WIDTH] = k
        kv_ref[:, (2 * g + 1) * PAIR_WIDTH:(2 * g + 2) * PAIR_WIDTH] = v


def _proj_sample(layer, x, p):
    rows = x.shape[0]
    in_specs = [
        _full((rows, D_MODEL)),
        _layer_block((1, D_MODEL), layer),
        _layer_block((D_MODEL, MAIN_COLS), layer),
        _layer_block((D_MODEL, 2 * D_MODEL), layer),
        _layer_block((1, A_WIDTH), layer),
        _layer_block((1, PAIR_WIDTH), layer),
        _layer_block((1, PAIR_WIDTH), layer),
        _full((rows, PAIR_WIDTH)),
        _full((rows, PAIR_WIDTH)),
        _full((PAIR_WIDTH, PAIR_WIDTH)),
    ]
    widths = [A_WIDTH, A_WIDTH, B_WIDTH, 2 * B_WIDTH, D_MODEL, D_MODEL]
    return pl.pallas_call(
        _sample_proj_kernel,
        grid=(1,),
        in_specs=in_specs,
        out_specs=[_full((rows, w)) for w in widths],
        out_shape=[jax.ShapeDtypeStruct((rows, w), F32) for w in widths],
        compiler_params=_params(),
        name=f"proj_sample_l{layer}",
    )(x, p["g_mix"], p["w_main"], p["w_gates"], p["g_v"], p["g_q"], p["g_k"],
      p["cos_s"], p["sin_s"], p["mean_mat"])


def _sample_attn_kernel(qt_ref, kvt_ref, c0_ref, c1_ref, c2_ref, sel1_ref, sel2_ref,
                        out_ref, o_scr, l_scr):
    kv_rows = 2 * HEADS_PER_PAIR * HEAD_DIM
    for g, (c_ref, sel_ref) in enumerate(((c0_ref, None), (c1_ref, sel1_ref),
                                          (c2_ref, sel2_ref))):
        window, dil = DILATED_PAIRS[g]
        for b in range(SAMPLE_GROUP):
            col = slice(b, b + 1)
            if sel_ref is not None:
                blk = c_ref[0, b].reshape(kv_rows, window).astype(BF16)
                taps = jnp.dot(blk, sel_ref[...], preferred_element_type=F32)
            for h in range(HEADS_PER_PAIR):
                if sel_ref is None:
                    kt, vt = c_ref[0, b, 0, h], c_ref[0, b, 1, h]
                else:
                    kt = taps[h * HEAD_DIM:(h + 1) * HEAD_DIM]
                    vt = taps[(HEADS_PER_PAIR + h) * HEAD_DIM:(HEADS_PER_PAIR + h + 1) * HEAD_DIM]
                feat = slice(g * PAIR_WIDTH + h * HEAD_DIM, g * PAIR_WIDTH + (h + 1) * HEAD_DIM)
                kfeat = slice(2 * g * PAIR_WIDTH + h * HEAD_DIM,
                              2 * g * PAIR_WIDTH + (h + 1) * HEAD_DIM)
                vfeat = slice((2 * g + 1) * PAIR_WIDTH + h * HEAD_DIM,
                              (2 * g + 1) * PAIR_WIDTH + (h + 1) * HEAD_DIM)
                q = qt_ref[0, feat, col]
                k_new = kvt_ref[0, kfeat, col]
                v_new = kvt_ref[0, vfeat, col]
                s = jnp.sum(kt * q, axis=0, keepdims=True)
                s_new = jnp.sum(k_new * q, axis=0, keepdims=True)
                mx = jnp.maximum(jnp.max(s, axis=1, keepdims=True), s_new)
                pr = jnp.exp(s - mx)
                p_new = jnp.exp(s_new - mx)
                den = jnp.sum(pr, axis=1, keepdims=True) + p_new
                o = (jnp.sum(vt * pr, axis=1, keepdims=True) + p_new * v_new) / den
                out_rows = slice(h * HEAD_DIM, (h + 1) * HEAD_DIM)
                o_scr[g, out_rows, col] = o
                l_scr[g, out_rows, col] = jnp.broadcast_to(mx + jnp.log(den), (HEAD_DIM, 1))
    l0, l1, l2 = l_scr[0], l_scr[1], l_scr[2]
    m = jnp.maximum(jnp.maximum(l0, l1), l2)
    e0, e1, e2 = jnp.exp(l0 - m), jnp.exp(l1 - m), jnp.exp(l2 - m)
    den = e0 + e1 + e2
    out_ref[0] = (e0 / den) * o_scr[0] + (e1 / den) * o_scr[1] + (e2 / den) * o_scr[2]


def _attn_sample(layer, qt, kvt, caches, tap_select):
    n = SAMPLE_GROUP
    groups = qt.shape[0]
    in_specs = [pl.BlockSpec((1, B_WIDTH, n), lambda i: (i, 0, 0)),
                pl.BlockSpec((1, 2 * B_WIDTH, n), lambda i: (i, 0, 0))]
    for (window, dil) in DILATED_PAIRS:
        in_specs.append(pl.BlockSpec((1, n, 2, HEADS_PER_PAIR, HEAD_DIM, window),
                                     lambda i: (layer, i, 0, 0, 0, 0)))
    in_specs += [_full(sel.shape) for sel in tap_select]
    scr = pltpu.VMEM((N_PAIRS, PAIR_WIDTH, n), F32)
    return pl.pallas_call(
        _sample_attn_kernel,
        grid=(groups,),
        in_specs=in_specs,
        out_specs=pl.BlockSpec((1, PAIR_WIDTH, n), lambda i: (i, 0, 0)),
        out_shape=jax.ShapeDtypeStruct((groups, PAIR_WIDTH, n), F32),
        scratch_shapes=[scr, scr],
        compiler_params=_params(),
        name=f"attn_sample_l{layer}",
    )(qt, kvt, *caches, *tap_select)


def _sample_mix_kernel(x_ref, u_ref, va_ref, bo_ref, ga_ref, gb_ref, wdiag_ref, bias0_ref,
                       wa_ref, wb_ref, wo_ref, out_ref):
    a_out = u_ref[...] * (wdiag_ref[...] * va_ref[...] + bias0_ref[...])
    out_ref[...] = _gated_merge(x_ref[...], a_out, bo_ref[...], ga_ref[...], gb_ref[...],
                                wa_ref, wb_ref, wo_ref)


def _mix_sample(layer, x, u, va, b_out, ga, gb, p):
    rows = x.shape[0]
    in_specs = [_full((rows, D_MODEL)), _full((rows, A_WIDTH)), _full((rows, A_WIDTH)),
                _full((rows, PAIR_WIDTH)), _full((rows, D_MODEL)), _full((rows, D_MODEL)),
                _layer_block((1, A_WIDTH), layer), _layer_block((1, A_WIDTH), layer),
                _layer_block((A_WIDTH, D_MODEL), layer),
                _layer_block((PAIR_WIDTH, D_MODEL), layer),
                _layer_block((D_MODEL, D_MODEL), layer)]
    return pl.pallas_call(
        _sample_mix_kernel,
        grid=(1,),
        in_specs=in_specs,
        out_specs=_full((rows, D_MODEL)),
        out_shape=jax.ShapeDtypeStruct((rows, D_MODEL), F32),
        compiler_params=_params(),
        name=f"mix_sample_l{layer}",
    )(x, u, va, b_out, ga, gb, p["wdiag"], p["bias0"], p["w_a"], p["w_b"], p["w_o"])


def _sample_ffn_kernel(x_ref, g_ref, wg_ref, wv_ref, cwg_ref, cwv_ref, cbg_ref, cbv_ref,
                       h0g_ref, h0v_ref, h1g_ref, h1v_ref, wdown_ref,
                       out_ref, upg_ref, upv_ref):
    j = pl.program_id(0)
    x = x_ref[...]
    xn = _rms(x, g_ref[...]).astype(BF16)

    def up_and_conv(w_ref, cw_ref, cb_ref, h0_ref, h1_ref, up_ref):
        up = jnp.dot(xn, w_ref[...], preferred_element_type=F32)
        up_ref[...] = up
        return (cb_ref[...] + cw_ref[0:1, :] * h0_ref[...] + cw_ref[1:2, :] * h1_ref[...]
                + cw_ref[2:3, :] * up)

    c_gate = up_and_conv(wg_ref, cwg_ref, cbg_ref, h0g_ref, h1g_ref, upg_ref)
    c_val = up_and_conv(wv_ref, cwv_ref, cbv_ref, h0v_ref, h1v_ref, upv_ref)
    h = (jax.nn.silu(c_gate) * c_val).astype(BF16)
    part = jnp.dot(h, wdown_ref[...], preferred_element_type=F32)

    @pl.when(j == 0)
    def _():
        out_ref[...] = x + part

    @pl.when(j > 0)
    def _():
        out_ref[...] += part


def _ffn_sample(layer, x, p, state_conv):
    rows = x.shape[0]
    fc = SAMPLE_FF_CHUNK
    nch = D_FF // fc

    def cols(nrows, blk):
        return pl.BlockSpec((None, nrows, fc), lambda j: (layer, 0, blk * nch + j))

    state_flat = state_conv.reshape(DEPTH, rows, (CONV_W - 1) * 2 * D_FF)
    hist = [cols(rows, 2 * row + half) for row in range(CONV_W - 1) for half in range(2)]
    in_specs = [
        _full((rows, D_MODEL)),
        _layer_block((1, D_MODEL), layer),
        cols(D_MODEL, 0), cols(D_MODEL, 1),
        cols(CONV_W, 0), cols(CONV_W, 1),
        cols(1, 0), cols(1, 1),
        *hist,
        pl.BlockSpec((None, fc, D_MODEL), lambda j: (layer, j, 0)),
    ]
    up_spec = pl.BlockSpec((rows, fc), lambda j: (0, j))
    up_shape = jax.ShapeDtypeStruct((rows, D_FF), F32)
    return pl.pallas_call(
        _sample_ffn_kernel,
        grid=(nch,),
        in_specs=in_specs,
        out_specs=[_full((rows, D_MODEL)), up_spec, up_spec],
        out_shape=[jax.ShapeDtypeStruct((rows, D_MODEL), F32), up_shape, up_shape],
        compiler_params=_params(),
        name=f"ffn_sample_l{layer}",
    )(x, p["g_ffn"], p["w_up"], p["w_up"], p["conv_w"], p["conv_w"], p["conv_b"], p["conv_b"],
      state_flat, state_flat, state_flat, state_flat, p["w_down"])


def _sample_layer(layer, xs, p, caches_t, state_conv):
    n = SAMPLE_GROUP
    rows = xs.shape[0]
    u, va, q, kv, ga, gb = _proj_sample(layer, xs, p)

    def columns(a):
        return a.reshape(rows // n, n, a.shape[1]).swapaxes(1, 2)

    bt = _attn_sample(layer, columns(q), columns(kv), caches_t, p["tap_select"])
    b_out = bt.swapaxes(1, 2).reshape(rows, PAIR_WIDTH)
    xs = _mix_sample(layer, xs, u, va, b_out, ga, gb, p)
    xs, upg, upv = _ffn_sample(layer, xs, p, state_conv)
    up_row = jnp.concatenate([upg, upv], axis=-1)[:, None, :]
    new_conv = jnp.concatenate([state_conv[layer][:, 1:], up_row], axis=1)
    return xs, kv, va, new_conv


def _rope_tables(pos):
    half = HEAD_DIM // 2
    inv_freq = ROPE_THETA ** (-jnp.arange(half, dtype=F32) / half)
    ang = pos.astype(F32)[:, None] * inv_freq[None, :]
    cos, sin = jnp.cos(ang), jnp.sin(ang)
    cos_full = jnp.tile(jnp.concatenate([cos, cos], axis=1), (1, HEADS_PER_PAIR))
    sin_signed = jnp.tile(jnp.concatenate([-sin, sin], axis=1), (1, HEADS_PER_PAIR))
    return cos_full, sin_signed


def _prepare_params(g_mix, w_in, g_v, w_s, b_s, g_q, g_k, w_a_proj, w_b_proj, w_o,
                    g_ffn, w_up, conv_w, conv_b, w_down):
    head_of = jnp.arange(PAIR_WIDTH) // HEAD_DIM
    same_head = head_of[:, None] == head_of[None, :]
    bias_full = jnp.repeat(jnp.swapaxes(b_s, 1, 2), A_GROUP_DIM, axis=2)
    cos_p, sin_p = _rope_tables(jnp.arange(SEQ))
    cos_s, sin_s = _rope_tables(jnp.full((DEC_BATCH,), PAST_LEN))
    tap_select = [
        (jnp.arange(window)[:, None] == jnp.arange(TAPS)[None, :] * dil).astype(BF16)
        for (window, dil) in DILATED_PAIRS if dil > 1]
    return dict(
        tap_select=tap_select,
        g_mix=g_mix[:, None, :], g_v=g_v[:, None, :], g_ffn=g_ffn[:, None, :],
        g_q=jnp.tile(g_q, (1, HEADS_PER_PAIR))[:, None, :],
        g_k=jnp.tile(g_k, (1, HEADS_PER_PAIR))[:, None, :],
        w_main=w_in[:, :, :MAIN_COLS].astype(BF16),
        w_gates=w_in[:, :, OFF_GATES:].astype(BF16),
        w_a=w_a_proj.astype(BF16), w_b=w_b_proj.astype(BF16), w_o=w_o.astype(BF16),
        w_up=w_up.astype(BF16), w_down=w_down.astype(BF16),
        w_s=w_s, bias_full=bias_full, bias0=bias_full[:, 0:1, :],
        wdiag=jnp.repeat(w_s[:, :, 0, 0], A_GROUP_DIM, axis=1)[:, None, :],
        conv_w=conv_w, conv_b=conv_b[:, None, :],
        mean_mat=jnp.where(same_head, 1.0 / HEAD_DIM, 0.0).astype(BF16),
        cos_p=cos_p, sin_p=sin_p, cos_s=cos_s, sin_s=sin_s,
    )


def kernel(x_prompt, x_sample, cache_kv_w128, cache_kv_w512, cache_kv_w2048, state_conv,
           g_mix, w_in, g_v, w_s, b_s, g_q, g_k, w_a_proj, w_b_proj, w_o,
           g_ffn, w_up, conv_w, conv_b, w_down):
    assert x_prompt.shape == (BATCH, SEQ, D_MODEL) and x_sample.shape == (DEC_BATCH, 1, D_MODEL)
    p = _prepare_params(g_mix, w_in, g_v, w_s, b_s, g_q, g_k, w_a_proj, w_b_proj, w_o,
                        g_ffn, w_up, conv_w, conv_b, w_down)
    caches_t = []
    for cache, (window, dil) in zip((cache_kv_w128, cache_kv_w512, cache_kv_w2048), DILATED_PAIRS):
        assert cache.shape[2] == window
        caches_t.append(jnp.transpose(cache, (0, 1, 3, 4, 5, 2)))

    xp = x_prompt.reshape(BATCH * SEQ, D_MODEL)
    xs = x_sample.reshape(DEC_BATCH, D_MODEL)
    kv_prompt = None
    conv_p, conv_s, kv_s, va_s = [], [], [], []
    for l in range(DEPTH):
        outs = _proj_prompt(l, xp, p, kv_prompt)
        u, va = outs[0], outs[1]
        qkv = outs[2:11]
        kv_prompt = outs[11:14]
        b_out = _attn_prompt(l, qkv)
        xp = _mix_prompt(l, xp, u, va, b_out, p)
        xp, nconv = _ffn_prompt(l, xp, p)
        conv_p.append(nconv)

        xs, skv, sva, sconv = _sample_layer(l, xs, p, caches_t, state_conv)
        kv_s.append(skv)
        va_s.append(sva)
        conv_s.append(sconv)

    def kv_shape(a, rows):
        return a.reshape(DEPTH, -1, rows, 2, HEADS_PER_PAIR, HEAD_DIM)

    kv_s = jnp.stack(kv_s, axis=0)
    kv_s = [kv_s[:, :, 2 * g * PAIR_WIDTH:(2 * g + 2) * PAIR_WIDTH] for g in range(N_PAIRS)]
    return (
        xp.reshape(BATCH, SEQ, D_MODEL),
        xs.reshape(DEC_BATCH, 1, D_MODEL),
        kv_shape(kv_prompt[0], min(DILATED_PAIRS[0][0], SEQ)),
        kv_shape(kv_prompt[1], min(DILATED_PAIRS[1][0], SEQ)),
        kv_shape(kv_prompt[2], min(DILATED_PAIRS[2][0], SEQ)),
        jnp.stack(conv_p, axis=0),
        kv_shape(kv_s[0], 1),
        kv_shape(kv_s[1], 1),
        kv_shape(kv_s[2], 1),
        jnp.stack(conv_s, axis=0),
        jnp.stack(va_s, axis=0).reshape(DEPTH, DEC_BATCH, 1, A_WIDTH),
    )
```
